```python
import jax, jax.numpy as jnp
from jax import lax
import numpy as np

D_MODEL = 1024
BATCH = 2
SEQ = 8192
DEPTH = 2
DEC_BATCH = 128
DEC_SEQ = 4
PAST_LEN = 2048
PAGE_SIZE = 128

HEAD_DIM = 64
RET_HEADS = 6
RET_WIDTH = RET_HEADS * HEAD_DIM
SB_HEADS = 6
SB_WIDTH = SB_HEADS * HEAD_DIM
CONV_WIDTH = 256
CONV_K = 31
MIX_WIDTH = RET_WIDTH + SB_WIDTH + CONV_WIDTH
IN_WIDTH = 4 * RET_WIDTH + 4 * SB_WIDTH + 3 * CONV_WIDTH
RET_CHUNK = 128
SB_BLOCK = 128
SB_BIAS_INIT = -5.0
ROPE_BASE = 10000.0
EPS = 1e-6

kernel_name = 'hybrid_retention_stickbreak_conformer_step'


def rmsnorm(x, w):
    xf = x.astype(jnp.float32)
    y = xf * lax.rsqrt(jnp.mean(xf * xf, axis=-1, keepdims=True) + EPS)
    return (y * w.astype(jnp.float32)).astype(x.dtype)


def layernorm(x, w, b):
    xf = x.astype(jnp.float32)
    mu = jnp.mean(xf, axis=-1, keepdims=True)
    var = jnp.mean(jnp.square(xf - mu), axis=-1, keepdims=True)
    y = (xf - mu) * lax.rsqrt(var + EPS) * w.astype(jnp.float32) + b.astype(jnp.float32)
    return y.astype(x.dtype)


def head_groupnorm(o, w):
    B, T, H, d = o.shape
    of = o.astype(jnp.float32)
    mu = jnp.mean(of, axis=-1, keepdims=True)
    var = jnp.mean(jnp.square(of - mu), axis=-1, keepdims=True)
    y = ((of - mu) * lax.rsqrt(var + EPS)).reshape(B, T, H * d) * w.astype(jnp.float32)
    return y.astype(o.dtype)


def rope(x, pos):
    half = HEAD_DIM // 2
    inv = jnp.power(ROPE_BASE, -2.0 * jnp.arange(half, dtype=jnp.float32) / HEAD_DIM)
    ang = pos.astype(jnp.float32)[:, None] * inv[None, :]
    cos = jnp.cos(ang)[None, :, None, :]
    sin = jnp.sin(ang)[None, :, None, :]
    xf = x.astype(jnp.float32)
    x1, x2 = xf[..., :half], xf[..., half:]
    return jnp.concatenate([x1 * cos - x2 * sin, x1 * sin + x2 * cos], axis=-1).astype(x.dtype)


def retention(q, k, v, S0):
    B, T, H, d = q.shape
    dt = q.dtype
    C = RET_CHUNK if T % RET_CHUNK == 0 else T
    N = T // C
    lg = jnp.log(1.0 - jnp.power(2.0, -5.0 - jnp.arange(H, dtype=jnp.float32)))
    c = jnp.arange(C, dtype=jnp.float32)
    diff = c[:, None] - c[None, :]
    D = jnp.where(diff[None] >= 0, jnp.exp(jnp.maximum(diff, 0.0)[None] * lg[:, None, None]), 0.0).astype(dt)
    q_dec = jnp.exp((c[:, None] + 1.0) * lg[None, :]).astype(dt)
    k_dec = jnp.exp((C - 1.0 - c)[:, None] * lg[None, :]).astype(dt)
    c_dec = jnp.exp(C * lg).astype(dt)

    def to_chunks(a):
        return a.reshape(B, N, C, H, a.shape[-1]).transpose(1, 0, 2, 3, 4)

    def step(S, xs):
        qc, kc, vc = xs
        att = jnp.einsum('bchd,bshd->bhcs', qc, kc) * D[None]
        o = (jnp.einsum('bhcs,bshe->bche', att, vc)
             + jnp.einsum('bchd,bhde->bche', qc * q_dec[None, :, :, None], S))
        S = S * c_dec[None, :, None, None] + jnp.einsum('bchd,bche->bhde', kc * k_dec[None, :, :, None], vc)
        return S, o

    S, o = lax.scan(step, S0.astype(dt), (to_chunks(q), to_chunks(k), to_chunks(v)))
    o = o.transpose(1, 0, 2, 3, 4).reshape(B, T, H, v.shape[-1])
    return o, S


def stickbreak(q, k, v, bias, n_past):
    B, Tq, H, d = q.shape
    scale = d ** -0.5
    bf = bias.astype(jnp.float32)[None, :, None, None]
    outs = []
    for start in range(0, Tq, SB_BLOCK):
        end = min(start + SB_BLOCK, Tq)
        nk = n_past + end
        qb, kb, vb = q[:, start:end], k[:, :nk], v[:, :nk]
        qpos = n_past + jnp.arange(start, end)
        kpos = jnp.arange(nk)
        mask = kpos[None, :] < qpos[:, None]
        z = jnp.einsum('bqhd,bkhd->bhqk', qb, kb).astype(jnp.float32) * scale + bf
        log_stop = jnp.where(mask, jax.nn.log_sigmoid(-z), 0.0)
        after = lax.cumsum(log_stop, axis=3, reverse=True) - log_stop
        A = jnp.where(mask, jnp.exp(jax.nn.log_sigmoid(z) + after), 0.0)
        outs.append(jnp.einsum('bhqk,bkhd->bqhd', A.astype(v.dtype), vb))
    return jnp.concatenate(outs, axis=1)


def conv_module(a, b, buf, dw_w, dw_b, ln_w, ln_b, pw_w):
    u = a * jax.nn.sigmoid(b)
    full = jnp.concatenate([buf.astype(u.dtype), u], axis=1)
    y = lax.conv_general_dilated(full, dw_w.astype(u.dtype)[:, None, :], (1,), 'VALID',
                                 dimension_numbers=('NWC', 'WIO', 'NWC'),
                                 feature_group_count=u.shape[-1]) + dw_b
    y = jax.nn.silu(layernorm(y, ln_w, ln_b)) @ pw_w
    return y, full[:, -(CONV_K - 1):]


def mixer_layer(x, pos, S0, k_past, v_past, buf, n_past,
                norm_pre, w_in, ret_gn_w, sb_bias, dw_w, dw_b, ln_w, ln_b, pw_w, w_out, norm_post):
    B, T, _ = x.shape
    h = rmsnorm(x, norm_pre)
    p = h @ w_in
    sizes = [RET_WIDTH] * 4 + [SB_WIDTH] * 4 + [CONV_WIDTH] * 3
    idx = np.cumsum(sizes)[:-1].tolist()
    rq, rk, rv, rg, sq, sk, sv, sg, ca, cb, cg = jnp.split(p, idx, axis=-1)

    def heads(t):
        return t.reshape(B, T, -1, HEAD_DIM)

    rq = rope(heads(rq), pos)
    rk = rope(heads(rk), pos) * (HEAD_DIM ** -0.5)
    o_r, S = retention(rq, rk, heads(rv), S0)
    o_r = head_groupnorm(o_r, ret_gn_w) * jax.nn.silu(rg)

    sk_h, sv_h = heads(sk), heads(sv)
    if k_past is None:
        k_all, v_all = sk_h, sv_h
    else:
        k_all = jnp.concatenate([k_past.astype(sk_h.dtype), sk_h], axis=1)
        v_all = jnp.concatenate([v_past.astype(sv_h.dtype), sv_h], axis=1)
    o_s = stickbreak(heads(sq), k_all, v_all, sb_bias, n_past).reshape(B, T, SB_WIDTH) * jax.nn.silu(sg)

    o_c, buf_new = conv_module(ca, cb, buf, dw_w, dw_b, ln_w, ln_b, pw_w)
    o_c = o_c * jax.nn.silu(cg)

    y = jnp.concatenate([o_r, o_s, o_c], axis=-1) @ w_out
    return x + rmsnorm(y, norm_post), S, sk_h, sv_h, buf_new


def setup_inputs(seed: int = 0) -> dict:
    key = jax.random.key(seed)
    ks = jax.random.split(key, 20)
    n_pages = PAST_LEN // PAGE_SIZE
    n_used = DEC_BATCH * n_pages
    n_pool = n_used + n_used // 4
    f32 = jnp.float32
    nrm = lambda k, s: jax.random.normal(k, s, f32)
    page_table = jax.random.permutation(ks[0], n_pool)[:n_used].reshape(DEC_BATCH, n_pages).astype(jnp.int32)
    return {
        'x_prompt': nrm(ks[1], (BATCH, SEQ, D_MODEL)),
        'x_sample': nrm(ks[2], (DEC_BATCH, DEC_SEQ, D_MODEL)),
        'state_ret': nrm(ks[3], (DEPTH, DEC_BATCH, RET_HEADS, HEAD_DIM, HEAD_DIM)),
        'cache_k': nrm(ks[4], (DEPTH, n_pool, PAGE_SIZE, SB_HEADS, HEAD_DIM)),
        'cache_v': nrm(ks[5], (DEPTH, n_pool, PAGE_SIZE, SB_HEADS, HEAD_DIM)),
        'state_conv': 0.5 * nrm(ks[6], (DEPTH, DEC_BATCH, CONV_K - 1, CONV_WIDTH)),
        'page_table': page_table,
        'norm_pre': 1.0 + 0.05 * nrm(ks[7], (DEPTH, D_MODEL)),
        'w_in': nrm(ks[8], (DEPTH, D_MODEL, IN_WIDTH)) * D_MODEL ** -0.5,
        'ret_gn_w': 1.0 + 0.05 * nrm(ks[9], (DEPTH, RET_WIDTH)),
        'sb_bias': SB_BIAS_INIT + 0.1 * nrm(ks[17], (DEPTH, SB_HEADS)),
        'conv_dw_w': nrm(ks[10], (DEPTH, CONV_K, CONV_WIDTH)) * CONV_K ** -0.5,
        'conv_dw_b': 0.01 * nrm(ks[11], (DEPTH, CONV_WIDTH)),
        'conv_ln_w': 1.0 + 0.05 * nrm(ks[12], (DEPTH, CONV_WIDTH)),
        'conv_ln_b': 0.01 * nrm(ks[13], (DEPTH, CONV_WIDTH)),
        'conv_pw_w': nrm(ks[14], (DEPTH, CONV_WIDTH, CONV_WIDTH)) * CONV_WIDTH ** -0.5,
        'w_out': nrm(ks[15], (DEPTH, MIX_WIDTH, D_MODEL)) * MIX_WIDTH ** -0.5,
        'norm_post': 1.0 + 0.05 * nrm(ks[16], (DEPTH, D_MODEL)),
    }


def reference(x_prompt, x_sample, state_ret, cache_k, cache_v, state_conv, page_table,
              norm_pre, w_in, ret_gn_w, sb_bias, conv_dw_w, conv_dw_b, conv_ln_w, conv_ln_b,
              conv_pw_w, w_out, norm_post):
    Bp, Tp, _ = x_prompt.shape
    Bs, Ts, _ = x_sample.shape
    n_pages = page_table.shape[1]
    past_len = n_pages * cache_k.shape[2]
    pos_p = jnp.arange(Tp)
    pos_s = past_len + jnp.arange(Ts)
    xp, xs = x_prompt, x_sample
    ret_p, ret_s, kp_l, vp_l, ks_l, vs_l, cp_l, cs_l = [], [], [], [], [], [], [], []
    for l in range(DEPTH):
        w = (norm_pre[l], w_in[l], ret_gn_w[l], sb_bias[l], conv_dw_w[l], conv_dw_b[l],
             conv_ln_w[l], conv_ln_b[l], conv_pw_w[l], w_out[l], norm_post[l])
        S0p = jnp.zeros((Bp, RET_HEADS, HEAD_DIM, HEAD_DIM), xp.dtype)
        buf0p = jnp.zeros((Bp, CONV_K - 1, CONV_WIDTH), xp.dtype)
        xp, Sp, kp, vp, bp = mixer_layer(xp, pos_p, S0p, None, None, buf0p, 0, *w)
        k_past = cache_k[l][page_table].reshape(Bs, past_len, SB_HEADS, HEAD_DIM)
        v_past = cache_v[l][page_table].reshape(Bs, past_len, SB_HEADS, HEAD_DIM)
        xs, Ss, ksn, vsn, bs = mixer_layer(xs, pos_s, state_ret[l], k_past, v_past,
                                           state_conv[l], past_len, *w)
        ret_p.append(Sp); ret_s.append(Ss)
        kp_l.append(kp); vp_l.append(vp); ks_l.append(ksn); vs_l.append(vsn)
        cp_l.append(bp); cs_l.append(bs)
    return (xp, xs, jnp.stack(ret_p), jnp.stack(ret_s), jnp.stack(kp_l), jnp.stack(vp_l),
            jnp.stack(ks_l), jnp.stack(vs_l), jnp.stack(cp_l), jnp.stack(cs_l))
```

```python
import functools

import numpy as np
import jax
import jax.numpy as jnp
from jax import lax
from jax.experimental import pallas as pl
from jax.experimental.pallas import tpu as pltpu

F32 = jnp.float32
BF16 = jnp.bfloat16

HEAD_DIM = 64
PAIR = 2 * HEAD_DIM
RET_HEADS = 6
SB_HEADS = 6
RET_WIDTH = RET_HEADS * HEAD_DIM
SB_WIDTH = SB_HEADS * HEAD_DIM
CONV_WIDTH = 256
CONV_K = 31
IN_WIDTH = 4 * RET_WIDTH + 4 * SB_WIDTH + 3 * CONV_WIDTH
RET_CHUNK = 128
ROPE_BASE = 10000.0
EPS = 1e-6
SAMPLE_PAD = 8

OFF_RQ, OFF_RK, OFF_RV, OFF_RG = 0, 384, 768, 1152
OFF_SQ, OFF_SK, OFF_SV, OFF_SG = 1536, 1920, 2304, 2688
OFF_CA, OFF_CB, OFF_CG = 3072, 3328, 3584

ROW_TILE = 512
SB_TQ = 256
SB_TK = 256
CONV_TT = 512
CONV_HALO = 32
VMEM_LIMIT = 56 * 1024 * 1024


def _dot(a, b):
    return jnp.dot(a, b, preferred_element_type=F32)


def _dot_nt(a, b):
    return lax.dot_general(a, b, (((1,), (1,)), ((), ())), preferred_element_type=F32)


def _dot_tn(a, b):
    return lax.dot_general(a, b, (((0,), (0,)), ((), ())), preferred_element_type=F32)


def _sigmoid(x):
    return 1.0 / (1.0 + jnp.exp(-x))


def _silu(x):
    return x * _sigmoid(x)


def _split_hi_lo(x):
    hi = x.astype(BF16)
    lo = (x - hi.astype(F32)).astype(BF16)
    return hi, lo


def _inproj_kernel(x_ref, g_ref, w_ref, p_ref):
    x = x_ref[...]
    ms = jnp.mean(x * x, axis=-1, keepdims=True)
    h = (x * lax.rsqrt(ms + EPS) * g_ref[...]).astype(BF16)
    step = 768
    for c in range(0, IN_WIDTH, step):
        p_ref[:, c:c + step] = _dot(h, w_ref[:, c:c + step])


def _inproj(x2d, g, w_bf16):
    n, d = x2d.shape
    tm = min(ROW_TILE, n)
    return pl.pallas_call(
        _inproj_kernel,
        grid=(n // tm,),
        in_specs=[
            pl.BlockSpec((tm, d), lambda i: (i, 0)),
            pl.BlockSpec((1, d), lambda i: (0, 0)),
            pl.BlockSpec((d, IN_WIDTH), lambda i: (0, 0)),
        ],
        out_specs=pl.BlockSpec((tm, IN_WIDTH), lambda i: (i, 0)),
        out_shape=jax.ShapeDtypeStruct((n, IN_WIDTH), F32),
        compiler_params=pltpu.CompilerParams(
            dimension_semantics=("arbitrary",), vmem_limit_bytes=VMEM_LIMIT),
        name="inproj",
    )(x2d, g, w_bf16)


def _outproj_kernel(or_ref, os_ref, oc_ref, x_ref, w_ref, g_ref, o_ref):
    mix = jnp.concatenate([or_ref[...], os_ref[...], oc_ref[...]], axis=-1)
    y = _dot(mix, w_ref[...])
    ms = jnp.mean(y * y, axis=-1, keepdims=True)
    o_ref[...] = x_ref[...] + y * lax.rsqrt(ms + EPS) * g_ref[...]


def _outproj(o_r, o_s, o_c, x2d, w_bf16, g):
    n, d = x2d.shape
    tm = min(ROW_TILE, n)
    row = lambda i: (i, 0)
    fixed = lambda i: (0, 0)
    return pl.pallas_call(
        _outproj_kernel,
        grid=(n // tm,),
        in_specs=[
            pl.BlockSpec((tm, RET_WIDTH), row),
            pl.BlockSpec((tm, SB_WIDTH), row),
            pl.BlockSpec((tm, CONV_WIDTH), row),
            pl.BlockSpec((tm, d), row),
            pl.BlockSpec(w_bf16.shape, fixed),
            pl.BlockSpec((1, d), fixed),
        ],
        out_specs=pl.BlockSpec((tm, d), row),
        out_shape=jax.ShapeDtypeStruct((n, d), F32),
        compiler_params=pltpu.CompilerParams(
            dimension_semantics=("arbitrary",), vmem_limit_bytes=VMEM_LIMIT),
        name="outproj",
    )(o_r, o_s, o_c, x2d, w_bf16, g)


def _rope_tables(pos):
    half = HEAD_DIM // 2
    inv = jnp.power(ROPE_BASE, -2.0 * jnp.arange(half, dtype=F32) / HEAD_DIM)
    ang = pos.astype(F32)[:, None] * inv[None, :]
    cos, sin = jnp.cos(ang), jnp.sin(ang)
    cos_t = jnp.concatenate([cos, cos, cos, cos], axis=-1)
    sin_t = jnp.concatenate([-sin, sin, -sin, sin], axis=-1)
    return cos_t, sin_t


def _retention_tables(c_valid, rows, keys):
    h = np.arange(RET_HEADS, dtype=np.float64)
    lg = np.log(1.0 - np.power(2.0, -5.0 - h))
    c = np.arange(rows, dtype=np.float64)
    s = np.arange(keys, dtype=np.float64)
    diff = c[:, None] - s[None, :]
    ok = (diff >= 0) & (s[None, :] < c_valid) & (c[:, None] < c_valid)
    dmat = np.where(ok[None], np.exp(np.maximum(diff, 0.0)[None] * lg[:, None, None]), 0.0)
    q_dec = np.exp((c[:, None] + 1.0) * lg[None, :])
    k_dec = np.where(c[:, None] < c_valid, np.exp((c_valid - 1.0 - c)[:, None] * lg[None, :]), 0.0)
    c_dec = np.exp(c_valid * lg)[None, :]
    lanes = lambda t: np.repeat(t, HEAD_DIM, axis=-1)
    return (jnp.asarray(dmat, F32), jnp.asarray(lanes(q_dec), F32),
            jnp.asarray(lanes(k_dec), F32), jnp.asarray(lanes(c_dec), F32))


def _retention_kernel(rq_ref, rk_ref, rv_ref, rg_ref, cos_ref, sin_ref, d_ref, qdec_ref,
                      kdec_ref, cdec_ref, gn_ref, s0_ref, o_ref, s_out_ref, s_scr, *, key_rows):
    c_idx = pl.program_id(1)

    @pl.when(c_idx == 0)
    def _():
        s_scr[...] = s0_ref[...]

    rows = rq_ref.shape[0]
    lane = lax.broadcasted_iota(jnp.int32, (rows, PAIR), 1)
    first = lane < HEAD_DIM
    low_half = (lane % HEAD_DIM) < (HEAD_DIM // 2)
    bd_r = lax.broadcasted_iota(jnp.int32, (PAIR, PAIR), 0) < HEAD_DIM
    bd_c = lax.broadcasted_iota(jnp.int32, (PAIR, PAIR), 1) < HEAD_DIM
    block_diag = bd_r == bd_c
    cos_t = cos_ref[...]
    sin_t = sin_ref[...]

    def rope(x):
        swapped = jnp.where(low_half, pltpu.roll(x, PAIR - HEAD_DIM // 2, 1),
                            pltpu.roll(x, HEAD_DIM // 2, 1))
        return x * cos_t + swapped * sin_t

    def pad_keys(x):
        if key_rows == rows:
            return x
        return jnp.concatenate([x, jnp.zeros((key_rows - rows, PAIR), x.dtype)], axis=0)

    for p in range(RET_HEADS // 2):
        sl = slice(PAIR * p, PAIR * (p + 1))
        q = rope(rq_ref[:, sl])
        k = rope(rk_ref[:, sl]) * (HEAD_DIM ** -0.5)
        qb = q.astype(BF16)
        kb = pad_keys(k.astype(BF16))
        vb = pad_keys(rv_ref[:, sl].astype(BF16))
        zero = jnp.zeros_like(qb)
        att0 = _dot_nt(jnp.where(first, qb, zero), kb) * d_ref[2 * p]
        att1 = _dot_nt(jnp.where(first, zero, qb), kb) * d_ref[2 * p + 1]
        s_old = s_scr[p]
        o = jnp.where(first, _dot(att0.astype(BF16), vb), _dot(att1.astype(BF16), vb))
        o = o + _dot((q * qdec_ref[:, sl]).astype(BF16), s_old.astype(BF16))
        kd = pad_keys((k * kdec_ref[:, sl]).astype(BF16))
        kv = _dot_tn(kd, vb)
        s_scr[p] = s_old * cdec_ref[:, sl] + jnp.where(block_diag, kv, 0.0)

        inv_d = 1.0 / HEAD_DIM
        mu0 = jnp.sum(jnp.where(first, o, 0.0), axis=-1, keepdims=True) * inv_d
        mu1 = jnp.sum(jnp.where(first, 0.0, o), axis=-1, keepdims=True) * inv_d
        dlt = o - jnp.where(first, mu0, mu1)
        sq = dlt * dlt
        v0 = jnp.sum(jnp.where(first, sq, 0.0), axis=-1, keepdims=True) * inv_d
        v1 = jnp.sum(jnp.where(first, 0.0, sq), axis=-1, keepdims=True) * inv_d
        y = dlt * lax.rsqrt(jnp.where(first, v0, v1) + EPS) * gn_ref[:, sl]
        o_ref[:, sl] = (y * _silu(rg_ref[:, sl])).astype(o_ref.dtype)

    @pl.when(c_idx == pl.num_programs(1) - 1)
    def _():
        s_out_ref[...] = s_scr[...]


def _retention(p3, cos_t, sin_t, tables, gn_w, s0_bd, *, chunk, key_rows):
    b, t, _ = p3.shape
    nc = t // chunk
    dmat, q_dec, k_dec, c_dec = tables
    col = lambda j: pl.BlockSpec((None, chunk, RET_WIDTH), lambda i, c, j=j: (i, c, j))
    fixed2 = lambda a: pl.BlockSpec(a.shape, lambda i, c: (0, 0))
    state_spec = pl.BlockSpec((None, RET_HEADS // 2, PAIR, PAIR), lambda i, c: (i, 0, 0, 0))
    return pl.pallas_call(
        functools.partial(_retention_kernel, key_rows=key_rows),
        grid=(b, nc),
        in_specs=[
            col(0), col(1), col(2), col(3),
            pl.BlockSpec((chunk, PAIR), lambda i, c: (c, 0)),
            pl.BlockSpec((chunk, PAIR), lambda i, c: (c, 0)),
            pl.BlockSpec(dmat.shape, lambda i, c: (0, 0, 0)),
            fixed2(q_dec), fixed2(k_dec), fixed2(c_dec), fixed2(gn_w),
            state_spec,
        ],
        out_specs=[
            pl.BlockSpec((None, chunk, RET_WIDTH), lambda i, c: (i, c, 0)),
            state_spec,
        ],
        out_shape=[
            jax.ShapeDtypeStruct((b, t, RET_WIDTH), BF16),
            jax.ShapeDtypeStruct(s0_bd.shape, F32),
        ],
        scratch_shapes=[pltpu.VMEM((RET_HEADS // 2, PAIR, PAIR), F32)],
        compiler_params=pltpu.CompilerParams(
            dimension_semantics=("arbitrary", "arbitrary"), vmem_limit_bytes=VMEM_LIMIT),
        name="retention",
    )(p3, p3, p3, p3, cos_t, sin_t, dmat, q_dec, k_dec, c_dec, gn_w, s0_bd)


def _to_block_diag(s):
    b = s.shape[0]
    s = s.reshape(b, RET_HEADS // 2, 2, HEAD_DIM, HEAD_DIM)
    z = jnp.zeros_like(s[:, :, 0])
    top = jnp.concatenate([s[:, :, 0], z], axis=-1)
    bot = jnp.concatenate([z, s[:, :, 1]], axis=-1)
    return jnp.concatenate([top, bot], axis=-2)


def _from_block_diag(s_bd):
    b = s_bd.shape[0]
    a = s_bd[:, :, :HEAD_DIM, :HEAD_DIM]
    d = s_bd[:, :, HEAD_DIM:, HEAD_DIM:]
    return jnp.stack([a, d], axis=2).reshape(b, RET_HEADS, HEAD_DIM, HEAD_DIM)


def _softplus_parts(z):
    t = jnp.log(1.0 + jnp.exp(-jnp.abs(z)))
    return jnp.maximum(z, 0.0) + t, jnp.minimum(z, 0.0) - t


def _later_sum(sp, upper):
    hi, lo = _split_hi_lo(sp)
    n = sp.shape[0]
    both = _dot(jnp.concatenate([hi, lo], axis=0), upper)
    return both[:n] + both[n:]


def _strict_upper(n):
    r = lax.broadcasted_iota(jnp.int32, (n, n), 0)
    c = lax.broadcasted_iota(jnp.int32, (n, n), 1)
    return jnp.where(r > c, 1.0, 0.0).astype(BF16)


def _sb_prompt_kernel(bias_ref, q_ref, k_ref, v_ref, g_ref, o_ref, kb_scr, vb_scr, acc_scr, car_scr):
    hp = pl.program_id(1)
    qi = pl.program_id(2)

    @pl.when(qi == 0)
    def _():
        kb_scr[...] = k_ref[...].astype(BF16)
        vb_scr[...] = v_ref[...].astype(BF16)

    tq, tk = SB_TQ, SB_TK
    q = q_ref[...] * (HEAD_DIM ** -0.5)
    lane = lax.broadcasted_iota(jnp.int32, (tq, PAIR), 1)
    first = lane < HEAD_DIM
    zero = jnp.zeros((tq, PAIR), F32)
    qm = (jnp.where(first, q, zero).astype(BF16), jnp.where(first, zero, q).astype(BF16))
    upper = _strict_upper(tk)
    causal = (lax.broadcasted_iota(jnp.int32, (tq, tk), 1)
              < lax.broadcasted_iota(jnp.int32, (tq, tk), 0))
    acc_scr[...] = jnp.zeros_like(acc_scr)
    car_scr[...] = jnp.zeros_like(car_scr)

    def tile(j, diagonal):
        start = pl.multiple_of(j * tk, tk)
        kb = kb_scr[pl.ds(start, tk), :]
        vb = vb_scr[pl.ds(start, tk), :]
        for h in range(2):
            z = _dot_nt(qm[h], kb) + bias_ref[2 * hp + h]
            sp, lb = _softplus_parts(z)
            if diagonal:
                sp = jnp.where(causal, sp, 0.0)
            arg = lb - _later_sum(sp, upper) - car_scr[h]
            a = jnp.exp(arg)
            if diagonal:
                a = jnp.where(causal, a, 0.0)
            acc_scr[h] += _dot(a.astype(BF16), vb)
            car_scr[h] += jnp.sum(sp, axis=-1, keepdims=True)

    tile(qi, True)

    def body(i, carry):
        tile(qi - 1 - i, False)
        return carry

    lax.fori_loop(0, qi, body, 0)
    o = jnp.where(first, acc_scr[0], acc_scr[1])
    o_ref[...] = (o * _silu(g_ref[...])).astype(o_ref.dtype)


def _sb_prompt(p3, bias):
    b, t, _ = p3.shape
    assert SB_TQ == SB_TK and t % SB_TQ == 0
    nq = t // SB_TQ
    pairs = SB_HEADS // 2
    qcol = lambda off: pl.BlockSpec(
        (None, SB_TQ, PAIR), lambda i, hp, qi, off=off: (i, qi, off // PAIR + hp))
    kcol = lambda off: pl.BlockSpec(
        (None, t, PAIR), lambda i, hp, qi, off=off: (i, 0, off // PAIR + hp))
    return pl.pallas_call(
        _sb_prompt_kernel,
        grid=(b, pairs, nq),
        in_specs=[
            pl.BlockSpec(memory_space=pltpu.SMEM),
            qcol(OFF_SQ), kcol(OFF_SK), kcol(OFF_SV), qcol(OFF_SG),
        ],
        out_specs=pl.BlockSpec((None, SB_TQ, PAIR), lambda i, hp, qi: (i, qi, hp)),
        out_shape=jax.ShapeDtypeStruct((b, t, SB_WIDTH), BF16),
        scratch_shapes=[
            pltpu.VMEM((t, PAIR), BF16), pltpu.VMEM((t, PAIR), BF16),
            pltpu.VMEM((2, SB_TQ, PAIR), F32), pltpu.VMEM((2, SB_TQ, 1), F32),
        ],
        compiler_params=pltpu.CompilerParams(
            dimension_semantics=("arbitrary", "arbitrary", "arbitrary"),
            vmem_limit_bytes=VMEM_LIMIT),
        name="sb_prompt",
    )(bias, p3, p3, p3, p3)


def _sb_sample_kernel(pt_ref, bias_ref, q_ref, kn_ref, vn_ref, g_ref, *rest, n_pages, page):
    k_refs = rest[:n_pages]
    v_refs = rest[n_pages:2 * n_pages]
    o_ref = rest[2 * n_pages]
    rows = SAMPLE_PAD
    nrow = SB_HEADS * rows
    q = q_ref[...] * (HEAD_DIM ** -0.5)
    lane_head = lax.broadcasted_iota(jnp.int32, (rows, SB_WIDTH), 1) // HEAD_DIM
    zero = jnp.zeros((rows, SB_WIDTH), F32)
    qbd = jnp.concatenate(
        [jnp.where(lane_head == h, q, zero) for h in range(SB_HEADS)], axis=0).astype(BF16)
    row = lax.broadcasted_iota(jnp.int32, (nrow, page), 0)
    col = lax.broadcasted_iota(jnp.int32, (nrow, page), 1)
    bias = jnp.zeros((nrow, page), F32)
    for h in range(SB_HEADS):
        bias = jnp.where(row // rows == h, bias_ref[h], bias)
    causal = col < (row % rows)
    upper = _strict_upper(page)
    pad = jnp.zeros((page - rows, SB_WIDTH), BF16)

    keys = [jnp.concatenate([kn_ref[...].astype(BF16), pad], axis=0)]
    vals = [jnp.concatenate([vn_ref[...].astype(BF16), pad], axis=0)]
    for j in reversed(range(n_pages)):
        keys.append(k_refs[j][...].astype(BF16))
        vals.append(v_refs[j][...].astype(BF16))

    sps, lbs = [], []
    for i, kb in enumerate(keys):
        sp, lb = _softplus_parts(_dot_nt(qbd, kb) + bias)
        if i == 0:
            sp = jnp.where(causal, sp, 0.0)
        sps.append(sp)
        lbs.append(lb)
    halves = []
    for sp in sps:
        halves.extend(_split_hi_lo(sp))
    later = _dot(jnp.concatenate(halves, axis=0), upper)
    carry = jnp.zeros((nrow, 1), F32)
    acc = jnp.zeros((nrow, SB_WIDTH), F32)
    for i, (sp, lb, vb) in enumerate(zip(sps, lbs, vals)):
        within = later[2 * i * nrow:(2 * i + 1) * nrow] + later[(2 * i + 1) * nrow:(2 * i + 2) * nrow]
        a = jnp.exp(lb - within - carry)
        if i == 0:
            a = jnp.where(causal, a, 0.0)
        acc = acc + _dot(a.astype(BF16), vb)
        carry = carry + jnp.sum(sp, axis=-1, keepdims=True)
    out = jnp.zeros((rows, SB_WIDTH), F32)
    for h in range(SB_HEADS):
        out = out + jnp.where(lane_head == h, acc[h * rows:(h + 1) * rows], 0.0)
    o_ref[...] = (out * _silu(g_ref[...])).astype(o_ref.dtype)


def _sb_sample(p3, cache_k, cache_v, page_table, bias, layer):
    b = p3.shape[0]
    n_pages = page_table.shape[1]
    n_pool = cache_k.shape[0] // 2
    page = cache_k.shape[1]
    base = layer * n_pool
    tok = lambda off: pl.BlockSpec(
        (None, SAMPLE_PAD, SB_WIDTH), lambda i, pt, off=off: (i, 0, off // SB_WIDTH))
    page_spec = lambda j: pl.BlockSpec(
        (None, page, SB_WIDTH), lambda i, pt, j=j: (base + pt[i, j], 0, 0))
    grid_spec = pltpu.PrefetchScalarGridSpec(
        num_scalar_prefetch=1,
        grid=(b,),
        in_specs=[pl.BlockSpec(memory_space=pltpu.SMEM),
                  tok(OFF_SQ), tok(OFF_SK), tok(OFF_SV), tok(OFF_SG)]
        + [page_spec(j) for j in range(n_pages)] * 2,
        out_specs=pl.BlockSpec((None, SAMPLE_PAD, SB_WIDTH), lambda i, pt: (i, 0, 0)),
    )
    return pl.pallas_call(
        functools.partial(_sb_sample_kernel, n_pages=n_pages, page=page),
        grid_spec=grid_spec,
        out_shape=jax.ShapeDtypeStruct((b, SAMPLE_PAD, SB_WIDTH), BF16),
        compiler_params=pltpu.CompilerParams(
            dimension_semantics=("arbitrary",), vmem_limit_bytes=VMEM_LIMIT),
        name="sb_sample",
    )(page_table, bias, p3, p3, p3, p3, *([cache_k] * n_pages), *([cache_v] * n_pages))


def _conv_tail(y, lnw_ref, lnb_ref, pw_ref, gate):
    mu = jnp.mean(y, axis=-1, keepdims=True)
    d = y - mu
    var = jnp.mean(d * d, axis=-1, keepdims=True)
    z = d * lax.rsqrt(var + EPS) * lnw_ref[...] + lnb_ref[...]
    return _dot(_silu(z).astype(BF16), pw_ref[...]) * _silu(gate)


def _conv_prompt_kernel(ca_ref, cb_ref, cg_ref, dww_ref, dwb_ref, lnw_ref, lnb_ref, pw_ref,
                        o_ref, st_ref, ext_scr):
    ti = pl.program_id(1)
    tt = ca_ref.shape[0]

    @pl.when(ti == 0)
    def _():
        ext_scr[0:CONV_HALO, :] = jnp.zeros((CONV_HALO, CONV_WIDTH), F32)

    @pl.when(ti > 0)
    def _():
        ext_scr[0:CONV_HALO, :] = ext_scr[tt:tt + CONV_HALO, :]

    u = ca_ref[...] * _sigmoid(cb_ref[...])
    ext_scr[CONV_HALO:CONV_HALO + tt, :] = u
    first_tap = CONV_HALO - (CONV_K - 1)
    y = jnp.zeros((tt, CONV_WIDTH), F32) + dwb_ref[...]
    for j in range(CONV_K):
        y = y + ext_scr[first_tap + j:first_tap + j + tt, :] * dww_ref[j:j + 1, :]
    o_ref[...] = _conv_tail(y, lnw_ref, lnb_ref, pw_ref, cg_ref[...]).astype(o_ref.dtype)

    @pl.when(ti == pl.num_programs(1) - 1)
    def _():
        st_ref[...] = ext_scr[CONV_HALO + tt - (CONV_K - 1):CONV_HALO + tt, :]


def _conv_prompt(p3, dw_w, dw_b, ln_w, ln_b, pw_bf16):
    b, t, _ = p3.shape
    tt = min(CONV_TT, t)
    col = lambda off: pl.BlockSpec(
        (None, tt, CONV_WIDTH), lambda i, c, off=off: (i, c, off // CONV_WIDTH))
    fixed = lambda a: pl.BlockSpec(a.shape, lambda i, c: (0, 0))
    return pl.pallas_call(
        _conv_prompt_kernel,
        grid=(b, t // tt),
        in_specs=[col(OFF_CA), col(OFF_CB), col(OFF_CG),
                  fixed(dw_w), fixed(dw_b), fixed(ln_w), fixed(ln_b), fixed(pw_bf16)],
        out_specs=[
            pl.BlockSpec((None, tt, CONV_WIDTH), lambda i, c: (i, c, 0)),
            pl.BlockSpec((None, CONV_K - 1, CONV_WIDTH), lambda i, c: (i, 0, 0)),
        ],
        out_shape=[
            jax.ShapeDtypeStruct((b, t, CONV_WIDTH), BF16),
            jax.ShapeDtypeStruct((b, CONV_K - 1, CONV_WIDTH), F32),
        ],
        scratch_shapes=[pltpu.VMEM((tt + CONV_HALO, CONV_WIDTH), F32)],
        compiler_params=pltpu.CompilerParams(
            dimension_semantics=("arbitrary", "arbitrary"), vmem_limit_bytes=VMEM_LIMIT),
        name="conv_prompt",
    )(p3, p3, p3, dw_w, dw_b, ln_w, ln_b, pw_bf16)


def _conv_sample_kernel(ca_ref, cb_ref, cg_ref, buf_ref, dww_ref, dwb_ref, lnw_ref, lnb_ref,
                        pw_ref, o_ref, st_ref, *, n_tok):
    hist = CONV_K - 1
    rows = [buf_ref[:, j, :] for j in range(hist)]
    rows += [ca_ref[:, t, :] * _sigmoid(cb_ref[:, t, :]) for t in range(n_tok)]
    for t in range(n_tok):
        y = jnp.zeros_like(rows[0]) + dwb_ref[...]
        for j in range(CONV_K):
            y = y + rows[t + j] * dww_ref[j:j + 1, :]
        o_ref[:, t, :] = _conv_tail(y, lnw_ref, lnb_ref, pw_ref, cg_ref[:, t, :]).astype(o_ref.dtype)
    for t in range(n_tok, SAMPLE_PAD):
        o_ref[:, t, :] = jnp.zeros(rows[0].shape, o_ref.dtype)
    for j in range(hist):
        st_ref[:, j, :] = rows[n_tok + j]


def _conv_sample(p3, buf, dw_w, dw_b, ln_w, ln_b, pw_bf16, n_tok):
    b = p3.shape[0]
    bb = min(32, b)
    col = lambda off: pl.BlockSpec(
        (bb, SAMPLE_PAD, CONV_WIDTH), lambda i, off=off: (i, 0, off // CONV_WIDTH))
    fixed = lambda a: pl.BlockSpec(a.shape, lambda i: (0, 0))
    st_spec = pl.BlockSpec((bb, CONV_K - 1, CONV_WIDTH), lambda i: (i, 0, 0))
    return pl.pallas_call(
        functools.partial(_conv_sample_kernel, n_tok=n_tok),
        grid=(b // bb,),
        in_specs=[col(OFF_CA), col(OFF_CB), col(OFF_CG), st_spec,
                  fixed(dw_w), fixed(dw_b), fixed(ln_w), fixed(ln_b), fixed(pw_bf16)],
        out_specs=[pl.BlockSpec((bb, SAMPLE_PAD, CONV_WIDTH), lambda i: (i, 0, 0)), st_spec],
        out_shape=[
            jax.ShapeDtypeStruct((b, SAMPLE_PAD, CONV_WIDTH), BF16),
            jax.ShapeDtypeStruct(buf.shape, F32),
        ],
        compiler_params=pltpu.CompilerParams(
            dimension_semantics=("arbitrary",), vmem_limit_bytes=VMEM_LIMIT),
        name="conv_sample",
    )(p3, p3, p3, buf, dw_w, dw_b, ln_w, ln_b, pw_bf16)


def kernel(x_prompt, x_sample, state_ret, cache_k, cache_v, state_conv, page_table, norm_pre, w_in,
           ret_gn_w, sb_bias, conv_dw_w, conv_dw_b, conv_ln_w, conv_ln_b, conv_pw_w, w_out, norm_post):
    bp, tp, d = x_prompt.shape
    bs, ts, _ = x_sample.shape
    depth = w_in.shape[0]
    n_pages = page_table.shape[1]
    page = cache_k.shape[2]
    past_len = n_pages * page
    assert ts <= SAMPLE_PAD and tp % RET_CHUNK == 0

    xp = x_prompt.reshape(bp * tp, d)
    xs = jnp.pad(x_sample, ((0, 0), (0, SAMPLE_PAD - ts), (0, 0))).reshape(bs * SAMPLE_PAD, d)
    ck = cache_k.reshape(depth * cache_k.shape[1], page, SB_WIDTH)
    cv = cache_v.reshape(depth * cache_v.shape[1], page, SB_WIDTH)

    cos_p, sin_p = _rope_tables(jnp.arange(tp))
    cos_s, sin_s = _rope_tables(past_len + jnp.arange(SAMPLE_PAD))
    tab_p = _retention_tables(RET_CHUNK, RET_CHUNK, RET_CHUNK)
    tab_s = _retention_tables(ts, SAMPLE_PAD, PAIR)
    s0_p = jnp.zeros((bp, RET_HEADS // 2, PAIR, PAIR), F32)

    outs = [[] for _ in range(8)]
    for l in range(depth):
        w_in_b = w_in[l].astype(BF16)
        w_out_b = w_out[l].astype(BF16)
        pw_b = conv_pw_w[l].astype(BF16)
        g_pre, g_post = norm_pre[l][None, :], norm_post[l][None, :]
        gn_w = ret_gn_w[l][None, :]
        dw_b, ln_w, ln_b = conv_dw_b[l][None, :], conv_ln_w[l][None, :], conv_ln_b[l][None, :]

        pp = _inproj(xp, g_pre, w_in_b).reshape(bp, tp, IN_WIDTH)
        or_p, s_p = _retention(pp, cos_p, sin_p, tab_p, gn_w, s0_p, chunk=RET_CHUNK, key_rows=RET_CHUNK)
        os_p = _sb_prompt(pp, sb_bias[l])
        oc_p, cst_p = _conv_prompt(pp, conv_dw_w[l], dw_b, ln_w, ln_b, pw_b)
        xp = _outproj(or_p.reshape(bp * tp, -1), os_p.reshape(bp * tp, -1), oc_p.reshape(bp * tp, -1),
                      xp, w_out_b, g_post)

        ps = _inproj(xs, g_pre, w_in_b).reshape(bs, SAMPLE_PAD, IN_WIDTH)
        or_s, s_s = _retention(ps, cos_s, sin_s, tab_s, gn_w, _to_block_diag(state_ret[l]),
                               chunk=SAMPLE_PAD, key_rows=PAIR)
        os_s = _sb_sample(ps, ck, cv, page_table, sb_bias[l], l)
        oc_s, cst_s = _conv_sample(ps, state_conv[l], conv_dw_w[l], dw_b, ln_w, ln_b, pw_b, ts)
        n_s = bs * SAMPLE_PAD
        xs = _outproj(or_s.reshape(n_s, -1), os_s.reshape(n_s, -1), oc_s.reshape(n_s, -1),
                      xs, w_out_b, g_post)

        outs[0].append(_from_block_diag(s_p))
        outs[1].append(_from_block_diag(s_s))
        outs[2].append(pp[:, :, OFF_SK:OFF_SK + SB_WIDTH].reshape(bp, tp, SB_HEADS, HEAD_DIM))
        outs[3].append(pp[:, :, OFF_SV:OFF_SV + SB_WIDTH].reshape(bp, tp, SB_HEADS, HEAD_DIM))
        outs[4].append(ps[:, :ts, OFF_SK:OFF_SK + SB_WIDTH].reshape(bs, ts, SB_HEADS, HEAD_DIM))
        outs[5].append(ps[:, :ts, OFF_SV:OFF_SV + SB_WIDTH].reshape(bs, ts, SB_HEADS, HEAD_DIM))
        outs[6].append(cst_p)
        outs[7].append(cst_s)

    y_prompt = xp.reshape(bp, tp, d)
    y_sample = xs.reshape(bs, SAMPLE_PAD, d)[:, :ts]
    return (y_prompt, y_sample) + tuple(jnp.stack(o) for o in outs)
```

```python
import functools

import numpy as np
import jax
import jax.numpy as jnp
from jax import lax
from jax.experimental import pallas as pl
from jax.experimental.pallas import tpu as pltpu

F32 = jnp.float32
BF16 = jnp.bfloat16

HEAD_DIM = 64
PAIR = 2 * HEAD_DIM
RET_HEADS = 6
SB_HEADS = 6
RET_WIDTH = RET_HEADS * HEAD_DIM
SB_WIDTH = SB_HEADS * HEAD_DIM
CONV_WIDTH = 256
CONV_K = 31
IN_WIDTH = 4 * RET_WIDTH + 4 * SB_WIDTH + 3 * CONV_WIDTH
RET_CHUNK = 128
ROPE_BASE = 10000.0
EPS = 1e-6
LOG2_E = 1.4426950408889634
SAMPLE_PAD = 8

OFF_RQ, OFF_RK, OFF_RV, OFF_RG = 0, 384, 768, 1152
OFF_SQ, OFF_SK, OFF_SV, OFF_SG = 1536, 1920, 2304, 2688
OFF_CA, OFF_CB, OFF_CG = 3072, 3328, 3584

ROW_TILE = 512
SB_TQ = 256
SB_TK = 256
CONV_TT = 512
CONV_HALO = 32
VMEM_LIMIT = 56 * 1024 * 1024


def _dot(a, b):
    return jnp.dot(a, b, preferred_element_type=F32)


def _dot_nt(a, b):
    return lax.dot_general(a, b, (((1,), (1,)), ((), ())), preferred_element_type=F32)


def _dot_tn(a, b):
    return lax.dot_general(a, b, (((0,), (0,)), ((), ())), preferred_element_type=F32)


def _sigmoid(x):
    return 1.0 / (1.0 + jnp.exp(-x))


def _silu(x):
    return x * _sigmoid(x)


def _split_hi_lo(x):
    hi = x.astype(BF16)
    lo = (x - hi.astype(F32)).astype(BF16)
    return hi, lo


def _inproj_kernel(x_ref, g_ref, w_ref, p_ref):
    x = x_ref[...]
    ms = jnp.mean(x * x, axis=-1, keepdims=True)
    h = (x * lax.rsqrt(ms + EPS) * g_ref[...]).astype(BF16)
    step = 768
    for c in range(0, IN_WIDTH, step):
        p_ref[:, c:c + step] = _dot(h, w_ref[:, c:c + step])


def _inproj(x2d, g, w_bf16):
    n, d = x2d.shape
    tm = min(ROW_TILE, n)
    return pl.pallas_call(
        _inproj_kernel,
        grid=(n // tm,),
        in_specs=[
            pl.BlockSpec((tm, d), lambda i: (i, 0)),
            pl.BlockSpec((1, d), lambda i: (0, 0)),
            pl.BlockSpec((d, IN_WIDTH), lambda i: (0, 0)),
        ],
        out_specs=pl.BlockSpec((tm, IN_WIDTH), lambda i: (i, 0)),
        out_shape=jax.ShapeDtypeStruct((n, IN_WIDTH), F32),
        compiler_params=pltpu.CompilerParams(
            dimension_semantics=("arbitrary",), vmem_limit_bytes=VMEM_LIMIT),
        name="inproj",
    )(x2d, g, w_bf16)


def _outproj_kernel(or_ref, os_ref, oc_ref, x_ref, w_ref, g_ref, o_ref):
    mix = jnp.concatenate([or_ref[...], os_ref[...], oc_ref[...]], axis=-1)
    y = _dot(mix, w_ref[...])
    ms = jnp.mean(y * y, axis=-1, keepdims=True)
    o_ref[...] = x_ref[...] + y * lax.rsqrt(ms + EPS) * g_ref[...]


def _outproj(o_r, o_s, o_c, x2d, w_bf16, g):
    n, d = x2d.shape
    tm = min(ROW_TILE, n)
    row = lambda i: (i, 0)
    fixed = lambda i: (0, 0)
    return pl.pallas_call(
        _outproj_kernel,
        grid=(n // tm,),
        in_specs=[
            pl.BlockSpec((tm, RET_WIDTH), row),
            pl.BlockSpec((tm, SB_WIDTH), row),
            pl.BlockSpec((tm, CONV_WIDTH), row),
            pl.BlockSpec((tm, d), row),
            pl.BlockSpec(w_bf16.shape, fixed),
            pl.BlockSpec((1, d), fixed),
        ],
        out_specs=pl.BlockSpec((tm, d), row),
        out_shape=jax.ShapeDtypeStruct((n, d), F32),
        compiler_params=pltpu.CompilerParams(
            dimension_semantics=("arbitrary",), vmem_limit_bytes=VMEM_LIMIT),
        name="outproj",
    )(o_r, o_s, o_c, x2d, w_bf16, g)


def _rope_tables(pos):
    half = HEAD_DIM // 2
    inv = jnp.power(ROPE_BASE, -2.0 * jnp.arange(half, dtype=F32) / HEAD_DIM)
    ang = pos.astype(F32)[:, None] * inv[None, :]
    cos, sin = jnp.cos(ang), jnp.sin(ang)
    cos_t = jnp.concatenate([cos, cos, cos, cos], axis=-1)
    sin_t = jnp.concatenate([-sin, sin, -sin, sin], axis=-1)
    return cos_t, sin_t


def _retention_tables(c_valid, rows, keys):
    h = np.arange(RET_HEADS, dtype=np.float64)
    lg = np.log(1.0 - np.power(2.0, -5.0 - h))
    c = np.arange(rows, dtype=np.float64)
    s = np.arange(keys, dtype=np.float64)
    diff = c[:, None] - s[None, :]
    ok = (diff >= 0) & (s[None, :] < c_valid) & (c[:, None] < c_valid)
    dmat = np.where(ok[None], np.exp(np.maximum(diff, 0.0)[None] * lg[:, None, None]), 0.0)
    q_dec = np.exp((c[:, None] + 1.0) * lg[None, :])
    k_dec = np.where(c[:, None] < c_valid, np.exp((c_valid - 1.0 - c)[:, None] * lg[None, :]), 0.0)
    c_dec = np.exp(c_valid * lg)[None, :]
    lanes = lambda t: np.repeat(t, HEAD_DIM, axis=-1)
    return (jnp.asarray(dmat, F32), jnp.asarray(lanes(q_dec), F32),
            jnp.asarray(lanes(k_dec), F32), jnp.asarray(lanes(c_dec), F32))


def _retention_kernel(rq_ref, rk_ref, rv_ref, rg_ref, cos_ref, sin_ref, d_ref, qdec_ref,
                      kdec_ref, cdec_ref, gn_ref, s0_ref, o_ref, s_out_ref, s_scr, *, key_rows):
    c_idx = pl.program_id(1)

    @pl.when(c_idx == 0)
    def _():
        s_scr[...] = s0_ref[...]

    rows = rq_ref.shape[0]
    lane = lax.broadcasted_iota(jnp.int32, (rows, PAIR), 1)
    first = lane < HEAD_DIM
    low_half = (lane % HEAD_DIM) < (HEAD_DIM // 2)
    bd_r = lax.broadcasted_iota(jnp.int32, (PAIR, PAIR), 0) < HEAD_DIM
    bd_c = lax.broadcasted_iota(jnp.int32, (PAIR, PAIR), 1) < HEAD_DIM
    block_diag = bd_r == bd_c
    cos_t = cos_ref[...]
    sin_t = sin_ref[...]

    def rope(x):
        swapped = jnp.where(low_half, pltpu.roll(x, PAIR - HEAD_DIM // 2, 1),
                            pltpu.roll(x, HEAD_DIM // 2, 1))
        return x * cos_t + swapped * sin_t

    def pad_keys(x):
        if key_rows == rows:
            return x
        return jnp.concatenate([x, jnp.zeros((key_rows - rows, PAIR), x.dtype)], axis=0)

    for p in range(RET_HEADS // 2):
        sl = slice(PAIR * p, PAIR * (p + 1))
        q = rope(rq_ref[:, sl])
        k = rope(rk_ref[:, sl]) * (HEAD_DIM ** -0.5)
        qb = q.astype(BF16)
        kb = pad_keys(k.astype(BF16))
        vb = pad_keys(rv_ref[:, sl].astype(BF16))
        zero = jnp.zeros_like(qb)
        att0 = _dot_nt(jnp.where(first, qb, zero), kb) * d_ref[2 * p]
        att1 = _dot_nt(jnp.where(first, zero, qb), kb) * d_ref[2 * p + 1]
        s_old = s_scr[p]
        o = jnp.where(first, _dot(att0.astype(BF16), vb), _dot(att1.astype(BF16), vb))
        o = o + _dot((q * qdec_ref[:, sl]).astype(BF16), s_old.astype(BF16))
        kd = pad_keys((k * kdec_ref[:, sl]).astype(BF16))
        kv = _dot_tn(kd, vb)
        s_scr[p] = s_old * cdec_ref[:, sl] + jnp.where(block_diag, kv, 0.0)

        inv_d = 1.0 / HEAD_DIM
        mu0 = jnp.sum(jnp.where(first, o, 0.0), axis=-1, keepdims=True) * inv_d
        mu1 = jnp.sum(jnp.where(first, 0.0, o), axis=-1, keepdims=True) * inv_d
        dlt = o - jnp.where(first, mu0, mu1)
        sq = dlt * dlt
        v0 = jnp.sum(jnp.where(first, sq, 0.0), axis=-1, keepdims=True) * inv_d
        v1 = jnp.sum(jnp.where(first, 0.0, sq), axis=-1, keepdims=True) * inv_d
        y = dlt * lax.rsqrt(jnp.where(first, v0, v1) + EPS) * gn_ref[:, sl]
        o_ref[:, sl] = (y * _silu(rg_ref[:, sl])).astype(o_ref.dtype)

    @pl.when(c_idx == pl.num_programs(1) - 1)
    def _():
        s_out_ref[...] = s_scr[...]


def _retention(p3, cos_t, sin_t, tables, gn_w, s0_bd, *, chunk, key_rows):
    b, t, _ = p3.shape
    nc = t // chunk
    dmat, q_dec, k_dec, c_dec = tables
    col = lambda j: pl.BlockSpec((None, chunk, RET_WIDTH), lambda i, c, j=j: (i, c, j))
    fixed2 = lambda a: pl.BlockSpec(a.shape, lambda i, c: (0, 0))
    state_spec = pl.BlockSpec((None, RET_HEADS // 2, PAIR, PAIR), lambda i, c: (i, 0, 0, 0))
    return pl.pallas_call(
        functools.partial(_retention_kernel, key_rows=key_rows),
        grid=(b, nc),
        in_specs=[
            col(0), col(1), col(2), col(3),
            pl.BlockSpec((chunk, PAIR), lambda i, c: (c, 0)),
            pl.BlockSpec((chunk, PAIR), lambda i, c: (c, 0)),
            pl.BlockSpec(dmat.shape, lambda i, c: (0, 0, 0)),
            fixed2(q_dec), fixed2(k_dec), fixed2(c_dec), fixed2(gn_w),
            state_spec,
        ],
        out_specs=[
            pl.BlockSpec((None, chunk, RET_WIDTH), lambda i, c: (i, c, 0)),
            state_spec,
        ],
        out_shape=[
            jax.ShapeDtypeStruct((b, t, RET_WIDTH), BF16),
            jax.ShapeDtypeStruct(s0_bd.shape, F32),
        ],
        scratch_shapes=[pltpu.VMEM((RET_HEADS // 2, PAIR, PAIR), F32)],
        compiler_params=pltpu.CompilerParams(
            dimension_semantics=("arbitrary", "arbitrary"), vmem_limit_bytes=VMEM_LIMIT),
        name="retention",
    )(p3, p3, p3, p3, cos_t, sin_t, dmat, q_dec, k_dec, c_dec, gn_w, s0_bd)


def _to_block_diag(s):
    b = s.shape[0]
    s = s.reshape(b, RET_HEADS // 2, 2, HEAD_DIM, HEAD_DIM)
    z = jnp.zeros_like(s[:, :, 0])
    top = jnp.concatenate([s[:, :, 0], z], axis=-1)
    bot = jnp.concatenate([z, s[:, :, 1]], axis=-1)
    return jnp.concatenate([top, bot], axis=-2)


def _from_block_diag(s_bd):
    b = s_bd.shape[0]
    a = s_bd[:, :, :HEAD_DIM, :HEAD_DIM]
    d = s_bd[:, :, HEAD_DIM:, HEAD_DIM:]
    return jnp.stack([a, d], axis=2).reshape(b, RET_HEADS, HEAD_DIM, HEAD_DIM)


def _softplus_parts(z):
    sp = jnp.maximum(z, 0.0) + jnp.log(1.0 + jnp.exp2(jnp.abs(z) * (-LOG2_E)))
    return sp, z - sp


def _later_sum(sp, upper):
    hi, lo = _split_hi_lo(sp)
    return _dot(jnp.concatenate([hi, lo], axis=1), jnp.concatenate([upper, upper], axis=0))


def _strict_upper(n):
    r = lax.broadcasted_iota(jnp.int32, (n, n), 0)
    c = lax.broadcasted_iota(jnp.int32, (n, n), 1)
    return jnp.where(r > c, 1.0, 0.0).astype(BF16)


def _sb_prompt_kernel(bias_ref, q_ref, k_ref, v_ref, g_ref, o_ref, kb_scr, vb_scr, acc_scr, car_scr,
                      za_scr, zb_scr, w_scr):
    hp = pl.program_id(1)
    qi = pl.program_id(2)

    @pl.when(qi == 0)
    def _():
        kb_scr[...] = k_ref[...].astype(BF16)
        vb_scr[...] = v_ref[...].astype(BF16)

    tq, tk = SB_TQ, SB_TK
    q = q_ref[...] * (HEAD_DIM ** -0.5)
    lane = lax.broadcasted_iota(jnp.int32, (tq, PAIR), 1)
    first = lane < HEAD_DIM
    zero = jnp.zeros((tq, PAIR), F32)
    q2 = jnp.concatenate([jnp.where(first, q, zero), jnp.where(first, zero, q)], axis=0).astype(BF16)
    row2 = lax.broadcasted_iota(jnp.int32, (2 * tq, 1), 0)
    bias2 = jnp.where(row2 < tq, bias_ref[2 * hp], bias_ref[2 * hp + 1])
    upper = _strict_upper(tk)
    causal = (lax.broadcasted_iota(jnp.int32, (2 * tq, tk), 1)
              < lax.broadcasted_iota(jnp.int32, (2 * tq, tk), 0) % tq)
    acc_scr[...] = jnp.zeros_like(acc_scr)
    car_scr[...] = jnp.zeros_like(car_scr)

    def rows_of(scr, j):
        if isinstance(j, int):
            return scr[j * tk:(j + 1) * tk, :]
        return scr[pl.ds(pl.multiple_of(j * tk, tk), tk), :]

    def scores(tiles):
        return [_dot_nt(q2, rows_of(kb_scr, j)) for j in tiles]

    def weights(zs, diagonal):
        carry = car_scr[...]
        sps, lbs, carries = [], [], []
        for z in zs:
            sp, lb = _softplus_parts(z + bias2)
            if diagonal:
                sp = jnp.where(causal, sp, 0.0)
            carries.append(carry)
            carry = carry + jnp.sum(sp, axis=-1, keepdims=True)
            sps.append(sp)
            lbs.append(lb)
        car_scr[...] = carry
        later = [_dot(sp.astype(BF16), upper) for sp in sps]
        out = []
        for lb, lt, cr in zip(lbs, later, carries):
            a = jnp.exp(lb - lt - cr)
            if diagonal:
                a = jnp.where(causal, a, 0.0)
            out.append(a.astype(BF16))
        return out

    def accumulate(ws, tiles):
        acc = acc_scr[...]
        for w, j in zip(ws, tiles):
            acc = acc + _dot(w, rows_of(vb_scr, j))
        acc_scr[...] = acc

    accumulate(weights(scores([qi]), True), [qi])

    @pl.when(qi % 2 == 1)
    def _():
        accumulate(weights(scores([qi - 1]), False), [qi - 1])

    n_pairs = qi // 2
    top = qi - qi % 2 - 1

    @pl.when(n_pairs > 0)
    def _():
        z_scrs = (za_scr, zb_scr)
        for z_ref, z in zip(z_scrs, scores([top, top - 1])):
            z_ref[0] = z
        w_scr[1] = jnp.zeros(w_scr.shape[1:], BF16)

        def body(i, carry):
            slot = i % 2
            j = top - 2 * i
            prev = [jnp.minimum(j + 2, top), jnp.minimum(j + 1, top)]
            nxt = [jnp.maximum(j - 2, 0), jnp.maximum(j - 3, 0)]
            accumulate([w_scr[1 - slot, 0], w_scr[1 - slot, 1]], prev)
            row_carry = car_scr[...]
            args = []
            for z_ref, j_next in zip(z_scrs, nxt):
                sp, lb = _softplus_parts(z_ref[slot] + bias2)
                later = _dot(sp.astype(BF16), upper)
                z_ref[1 - slot] = scores([j_next])[0]
                args.append((lb, later, row_carry))
                row_carry = row_carry + jnp.sum(sp, axis=-1, keepdims=True)
            car_scr[...] = row_carry
            for t, (lb, later, cr) in enumerate(args):
                w_scr[slot, t] = jnp.exp(lb - later - cr).astype(BF16)
            return carry

        lax.fori_loop(0, n_pairs, body, 0)
        last = (n_pairs - 1) % 2
        accumulate([w_scr[last, 0], w_scr[last, 1]], [1, 0])

    acc = acc_scr[...]
    o = jnp.where(first, acc[:tq], acc[tq:])
    o_ref[...] = (o * _silu(g_ref[...])).astype(o_ref.dtype)


def _sb_prompt(p3, bias):
    b, t, _ = p3.shape
    assert SB_TQ == SB_TK and t % SB_TQ == 0
    nq = t // SB_TQ
    pairs = SB_HEADS // 2
    qcol = lambda off: pl.BlockSpec(
        (None, SB_TQ, PAIR), lambda i, hp, qi, off=off: (i, qi, off // PAIR + hp))
    kcol = lambda off: pl.BlockSpec(
        (None, t, PAIR), lambda i, hp, qi, off=off: (i, 0, off // PAIR + hp))
    return pl.pallas_call(
        _sb_prompt_kernel,
        grid=(b, pairs, nq),
        in_specs=[
            pl.BlockSpec(memory_space=pltpu.SMEM),
            qcol(OFF_SQ), kcol(OFF_SK), kcol(OFF_SV), qcol(OFF_SG),
        ],
        out_specs=pl.BlockSpec((None, SB_TQ, PAIR), lambda i, hp, qi: (i, qi, hp)),
        out_shape=jax.ShapeDtypeStruct((b, t, SB_WIDTH), BF16),
        scratch_shapes=[
            pltpu.VMEM((t, PAIR), BF16), pltpu.VMEM((t, PAIR), BF16),
            pltpu.VMEM((2 * SB_TQ, PAIR), F32), pltpu.VMEM((2 * SB_TQ, 1), F32),
            pltpu.VMEM((2, 2 * SB_TQ, SB_TK), F32), pltpu.VMEM((2, 2 * SB_TQ, SB_TK), F32),
            pltpu.VMEM((2, 2, 2 * SB_TQ, SB_TK), BF16),
        ],
        compiler_params=pltpu.CompilerParams(
            dimension_semantics=("arbitrary", "arbitrary", "arbitrary"),
            vmem_limit_bytes=VMEM_LIMIT),
        name="sb_prompt",
    )(bias, p3, p3, p3, p3)


def _sb_sample_kernel(pt_ref, bias_ref, q_ref, kn_ref, vn_ref, g_ref, *rest, n_pages, page):
    k_refs = rest[:n_pages]
    v_refs = rest[n_pages:2 * n_pages]
    o_ref = rest[2 * n_pages]
    rows = SAMPLE_PAD
    nrow = SB_HEADS * rows
    q = q_ref[...] * (HEAD_DIM ** -0.5)
    lane_head = lax.broadcasted_iota(jnp.int32, (rows, SB_WIDTH), 1) // HEAD_DIM
    zero = jnp.zeros((rows, SB_WIDTH), F32)
    qbd = jnp.concatenate(
        [jnp.where(lane_head == h, q, zero) for h in range(SB_HEADS)], axis=0).astype(BF16)
    row = lax.broadcasted_iota(jnp.int32, (nrow, page), 0)
    col = lax.broadcasted_iota(jnp.int32, (nrow, page), 1)
    bias = jnp.zeros((nrow, page), F32)
    for h in range(SB_HEADS):
        bias = jnp.where(row // rows == h, bias_ref[h], bias)
    causal = col < (row % rows)
    upper = _strict_upper(page)
    pad = jnp.zeros((page - rows, SB_WIDTH), BF16)

    new_k = jnp.concatenate([kn_ref[...].astype(BF16), pad], axis=0)
    new_v = jnp.concatenate([vn_ref[...].astype(BF16), pad], axis=0)
    zs = [_dot_nt(qbd, new_k)]
    for j in reversed(range(n_pages)):
        zs.append(_dot(qbd, k_refs[j][...].astype(BF16)))

    sps, lbs = [], []
    for i, z in enumerate(zs):
        sp, lb = _softplus_parts(z + bias)
        if i == 0:
            sp = jnp.where(causal, sp, 0.0)
        sps.append(sp)
        lbs.append(lb)
    halves = []
    for sp in sps:
        halves.extend(_split_hi_lo(sp))
    later = _dot(jnp.concatenate(halves, axis=0), upper)
    carry = jnp.zeros((nrow, 1), F32)
    acc = jnp.zeros((nrow, SB_WIDTH), F32)
    for i, (sp, lb) in enumerate(zip(sps, lbs)):
        within = later[2 * i * nrow:(2 * i + 1) * nrow] + later[(2 * i + 1) * nrow:(2 * i + 2) * nrow]
        a = jnp.exp(lb - within - carry)
        if i == 0:
            acc = acc + _dot(jnp.where(causal, a, 0.0).astype(BF16), new_v)
        else:
            acc = acc + _dot_nt(a.astype(BF16), v_refs[n_pages - i][...].astype(BF16))
        carry = carry + jnp.sum(sp, axis=-1, keepdims=True)
    out = jnp.zeros((rows, SB_WIDTH), F32)
    for h in range(SB_HEADS):
        out = out + jnp.where(lane_head == h, acc[h * rows:(h + 1) * rows], 0.0)
    o_ref[...] = (out * _silu(g_ref[...])).astype(o_ref.dtype)


def _sb_sample(p3, cache_k, cache_v, page_table, bias, layer):
    b = p3.shape[0]
    n_pages = page_table.shape[1]
    page = cache_k.shape[3]
    tok = lambda off: pl.BlockSpec(
        (None, SAMPLE_PAD, SB_WIDTH), lambda i, pt, off=off: (i, 0, off // SB_WIDTH))
    page_spec = lambda j: pl.BlockSpec(
        (None, None, SB_WIDTH, page), lambda i, pt, j=j: (layer, pt[i, j], 0, 0))
    grid_spec = pltpu.PrefetchScalarGridSpec(
        num_scalar_prefetch=1,
        grid=(b,),
        in_specs=[pl.BlockSpec(memory_space=pltpu.SMEM),
                  tok(OFF_SQ), tok(OFF_SK), tok(OFF_SV), tok(OFF_SG)]
        + [page_spec(j) for j in range(n_pages)] * 2,
        out_specs=pl.BlockSpec((None, SAMPLE_PAD, SB_WIDTH), lambda i, pt: (i, 0, 0)),
    )
    return pl.pallas_call(
        functools.partial(_sb_sample_kernel, n_pages=n_pages, page=page),
        grid_spec=grid_spec,
        out_shape=jax.ShapeDtypeStruct((b, SAMPLE_PAD, SB_WIDTH), BF16),
        compiler_params=pltpu.CompilerParams(
            dimension_semantics=("arbitrary",), vmem_limit_bytes=VMEM_LIMIT),
        name="sb_sample",
    )(page_table, bias, p3, p3, p3, p3, *([cache_k] * n_pages), *([cache_v] * n_pages))


def _conv_tail(y, lnw_ref, lnb_ref, pw_ref, gate):
    mu = jnp.mean(y, axis=-1, keepdims=True)
    d = y - mu
    var = jnp.mean(d * d, axis=-1, keepdims=True)
    z = d * lax.rsqrt(var + EPS) * lnw_ref[...] + lnb_ref[...]
    return _dot(_silu(z).astype(BF16), pw_ref[...]) * _silu(gate)


def _conv_prompt_kernel(ca_ref, cb_ref, cg_ref, dww_ref, dwb_ref, lnw_ref, lnb_ref, pw_ref,
                        o_ref, st_ref, ext_scr):
    ti = pl.program_id(1)
    tt = ca_ref.shape[0]

    @pl.when(ti == 0)
    def _():
        ext_scr[0:CONV_HALO, :] = jnp.zeros((CONV_HALO, CONV_WIDTH), F32)

    @pl.when(ti > 0)
    def _():
        ext_scr[0:CONV_HALO, :] = ext_scr[tt:tt + CONV_HALO, :]

    u = ca_ref[...] * _sigmoid(cb_ref[...])
    ext_scr[CONV_HALO:CONV_HALO + tt, :] = u
    first_tap = CONV_HALO - (CONV_K - 1)
    y = jnp.zeros((tt, CONV_WIDTH), F32) + dwb_ref[...]
    for j in range(CONV_K):
        y = y + ext_scr[first_tap + j:first_tap + j + tt, :] * dww_ref[j:j + 1, :]
    o_ref[...] = _conv_tail(y, lnw_ref, lnb_ref, pw_ref, cg_ref[...]).astype(o_ref.dtype)

    @pl.when(ti == pl.num_programs(1) - 1)
    def _():
        st_ref[...] = ext_scr[CONV_HALO + tt - (CONV_K - 1):CONV_HALO + tt, :]


def _conv_prompt(p3, dw_w, dw_b, ln_w, ln_b, pw_bf16):
    b, t, _ = p3.shape
    tt = min(CONV_TT, t)
    col = lambda off: pl.BlockSpec(
        (None, tt, CONV_WIDTH), lambda i, c, off=off: (i, c, off // CONV_WIDTH))
    fixed = lambda a: pl.BlockSpec(a.shape, lambda i, c: (0, 0))
    return pl.pallas_call(
        _conv_prompt_kernel,
        grid=(b, t // tt),
        in_specs=[col(OFF_CA), col(OFF_CB), col(OFF_CG),
                  fixed(dw_w), fixed(dw_b), fixed(ln_w), fixed(ln_b), fixed(pw_bf16)],
        out_specs=[
            pl.BlockSpec((None, tt, CONV_WIDTH), lambda i, c: (i, c, 0)),
            pl.BlockSpec((None, CONV_K - 1, CONV_WIDTH), lambda i, c: (i, 0, 0)),
        ],
        out_shape=[
            jax.ShapeDtypeStruct((b, t, CONV_WIDTH), BF16),
            jax.ShapeDtypeStruct((b, CONV_K - 1, CONV_WIDTH), F32),
        ],
        scratch_shapes=[pltpu.VMEM((tt + CONV_HALO, CONV_WIDTH), F32)],
        compiler_params=pltpu.CompilerParams(
            dimension_semantics=("arbitrary", "arbitrary"), vmem_limit_bytes=VMEM_LIMIT),
        name="conv_prompt",
    )(p3, p3, p3, dw_w, dw_b, ln_w, ln_b, pw_bf16)


def _conv_sample_kernel(ca_ref, cb_ref, cg_ref, buf_ref, dww_ref, dwb_ref, lnw_ref, lnb_ref,
                        pw_ref, o_ref, st_ref, *, n_tok):
    hist = CONV_K - 1
    rows = [buf_ref[:, j, :] for j in range(hist)]
    rows += [ca_ref[:, t, :] * _sigmoid(cb_ref[:, t, :]) for t in range(n_tok)]
    for t in range(n_tok):
        y = jnp.zeros_like(rows[0]) + dwb_ref[...]
        for j in range(CONV_K):
            y = y + rows[t + j] * dww_ref[j:j + 1, :]
        o_ref[:, t, :] = _conv_tail(y, lnw_ref, lnb_ref, pw_ref, cg_ref[:, t, :]).astype(o_ref.dtype)
    for t in range(n_tok, SAMPLE_PAD):
        o_ref[:, t, :] = jnp.zeros(rows[0].shape, o_ref.dtype)
    for j in range(hist):
        st_ref[:, j, :] = rows[n_tok + j]


def _conv_sample(p3, buf, dw_w, dw_b, ln_w, ln_b, pw_bf16, n_tok):
    b = p3.shape[0]
    bb = min(32, b)
    col = lambda off: pl.BlockSpec(
        (bb, SAMPLE_PAD, CONV_WIDTH), lambda i, off=off: (i, 0, off // CONV_WIDTH))
    fixed = lambda a: pl.BlockSpec(a.shape, lambda i: (0, 0))
    st_spec = pl.BlockSpec((bb, CONV_K - 1, CONV_WIDTH), lambda i: (i, 0, 0))
    return pl.pallas_call(
        functools.partial(_conv_sample_kernel, n_tok=n_tok),
        grid=(b // bb,),
        in_specs=[col(OFF_CA), col(OFF_CB), col(OFF_CG), st_spec,
                  fixed(dw_w), fixed(dw_b), fixed(ln_w), fixed(ln_b), fixed(pw_bf16)],
        out_specs=[pl.BlockSpec((bb, SAMPLE_PAD, CONV_WIDTH), lambda i: (i, 0, 0)), st_spec],
        out_shape=[
            jax.ShapeDtypeStruct((b, SAMPLE_PAD, CONV_WIDTH), BF16),
            jax.ShapeDtypeStruct(buf.shape, F32),
        ],
        compiler_params=pltpu.CompilerParams(
            dimension_semantics=("arbitrary",), vmem_limit_bytes=VMEM_LIMIT),
        name="conv_sample",
    )(p3, p3, p3, buf, dw_w, dw_b, ln_w, ln_b, pw_bf16)


def kernel(x_prompt, x_sample, state_ret, cache_k, cache_v, state_conv, page_table, norm_pre, w_in,
           ret_gn_w, sb_bias, conv_dw_w, conv_dw_b, conv_ln_w, conv_ln_b, conv_pw_w, w_out, norm_post):
    bp, tp, d = x_prompt.shape
    bs, ts, _ = x_sample.shape
    depth = w_in.shape[0]
    n_pages = page_table.shape[1]
    page = cache_k.shape[2]
    past_len = n_pages * page
    assert ts <= SAMPLE_PAD and tp % RET_CHUNK == 0

    xp = x_prompt.reshape(bp * tp, d)
    xs = jnp.pad(x_sample, ((0, 0), (0, SAMPLE_PAD - ts), (0, 0))).reshape(bs * SAMPLE_PAD, d)
    ck = cache_k.transpose(0, 1, 3, 4, 2).reshape(depth, cache_k.shape[1], SB_WIDTH, page)
    cv = cache_v.transpose(0, 1, 3, 4, 2).reshape(depth, cache_v.shape[1], SB_WIDTH, page)

    cos_p, sin_p = _rope_tables(jnp.arange(tp))
    cos_s, sin_s = _rope_tables(past_len + jnp.arange(SAMPLE_PAD))
    tab_p = _retention_tables(RET_CHUNK, RET_CHUNK, RET_CHUNK)
    tab_s = _retention_tables(ts, SAMPLE_PAD, PAIR)
    s0_p = jnp.zeros((bp, RET_HEADS // 2, PAIR, PAIR), F32)

    outs = [[] for _ in range(8)]
    for l in range(depth):
        w_in_b = w_in[l].astype(BF16)
        w_out_b = w_out[l].astype(BF16)
        pw_b = conv_pw_w[l].astype(BF16)
        g_pre, g_post = norm_pre[l][None, :], norm_post[l][None, :]
        gn_w = ret_gn_w[l][None, :]
        dw_b, ln_w, ln_b = conv_dw_b[l][None, :], conv_ln_w[l][None, :], conv_ln_b[l][None, :]

        pp = _inproj(xp, g_pre, w_in_b).reshape(bp, tp, IN_WIDTH)
        or_p, s_p = _retention(pp, cos_p, sin_p, tab_p, gn_w, s0_p, chunk=RET_CHUNK, key_rows=RET_CHUNK)
        os_p = _sb_prompt(pp, sb_bias[l])
        oc_p, cst_p = _conv_prompt(pp, conv_dw_w[l], dw_b, ln_w, ln_b, pw_b)
        xp = _outproj(or_p.reshape(bp * tp, -1), os_p.reshape(bp * tp, -1), oc_p.reshape(bp * tp, -1),
                      xp, w_out_b, g_post)

        ps = _inproj(xs, g_pre, w_in_b).reshape(bs, SAMPLE_PAD, IN_WIDTH)
        or_s, s_s = _retention(ps, cos_s, sin_s, tab_s, gn_w, _to_block_diag(state_ret[l]),
                               chunk=SAMPLE_PAD, key_rows=PAIR)
        os_s = _sb_sample(ps, ck, cv, page_table, sb_bias[l], l)
        oc_s, cst_s = _conv_sample(ps, state_conv[l], conv_dw_w[l], dw_b, ln_w, ln_b, pw_b, ts)
        n_s = bs * SAMPLE_PAD
        xs = _outproj(or_s.reshape(n_s, -1), os_s.reshape(n_s, -1), oc_s.reshape(n_s, -1),
                      xs, w_out_b, g_post)

        outs[0].append(_from_block_diag(s_p))
        outs[1].append(_from_block_diag(s_s))
        outs[2].append(pp[:, :, OFF_SK:OFF_SK + SB_WIDTH].reshape(bp, tp, SB_HEADS, HEAD_DIM))
        outs[3].append(pp[:, :, OFF_SV:OFF_SV + SB_WIDTH].reshape(bp, tp, SB_HEADS, HEAD_DIM))
        outs[4].append(ps[:, :ts, OFF_SK:OFF_SK + SB_WIDTH].reshape(bs, ts, SB_HEADS, HEAD_DIM))
        outs[5].append(ps[:, :ts, OFF_SV:OFF_SV + SB_WIDTH].reshape(bs, ts, SB_HEADS, HEAD_DIM))
        outs[6].append(cst_p)
        outs[7].append(cst_s)

    y_prompt = xp.reshape(bp, tp, d)
    y_sample = xs.reshape(bs, SAMPLE_PAD, d)[:, :ts]
    return (y_prompt, y_sample) + tuple(jnp.stack(o) for o in outs)
```

```python
import functools

import numpy as np
import jax
import jax.numpy as jnp
from jax import lax
from jax.experimental import pallas as pl
from jax.experimental.pallas import tpu as pltpu

F32 = jnp.float32
BF16 = jnp.bfloat16

HEAD_DIM = 64
PAIR = 2 * HEAD_DIM
RET_HEADS = 6
SB_HEADS = 6
RET_WIDTH = RET_HEADS * HEAD_DIM
SB_WIDTH = SB_HEADS * HEAD_DIM
CONV_WIDTH = 256
CONV_K = 31
IN_WIDTH = 4 * RET_WIDTH + 4 * SB_WIDTH + 3 * CONV_WIDTH
RET_CHUNK = 128
ROPE_BASE = 10000.0
EPS = 1e-6
LOG2_E = 1.4426950408889634
SUBLANES = 8
SAMPLE_PAD = SUBLANES

A_WIDTH, KV_WIDTH, B_WIDTH = 1920, 768, 1152
A_SQ = 1536
KV_SK, KV_SV = 0, 384
B_CA, B_CB, B_CG, B_SG = 0, 256, 512, 768
INPROJ_CHUNKS = ((0, 768, 0, 0), (768, 768, 0, 768), (1536, 384, 0, 1536),
                 (1920, 768, 1, 0), (3072, 768, 2, 0), (2688, 384, 2, 768))

ROW_TILE = 512
SB_TQ = 256
SB_TK = 256
SB_SPLIT = 2
RET_NB_PROMPT = 2
RET_NB_SAMPLE = 8
CONV_TT = 512
CONV_HALO = 32
VMEM_LIMIT = 56 * 1024 * 1024


def _dot(a, b):
    return jnp.dot(a, b, preferred_element_type=F32)


def _dot_nt(a, b):
    return lax.dot_general(a, b, (((1,), (1,)), ((), ())), preferred_element_type=F32)


def _dot_tn(a, b):
    return lax.dot_general(a, b, (((0,), (0,)), ((), ())), preferred_element_type=F32)


def _sigmoid(x):
    return 1.0 / (1.0 + jnp.exp(-x))


def _silu(x):
    return x * _sigmoid(x)


def _split_hi_lo(x):
    hi = x.astype(BF16)
    lo = (x - hi.astype(F32)).astype(BF16)
    return hi, lo


def _inproj_kernel(x_ref, g_ref, w_ref, *out_refs):
    x = x_ref[...]
    ms = jnp.mean(x * x, axis=-1, keepdims=True)
    h = (x * lax.rsqrt(ms + EPS) * g_ref[...]).astype(BF16)
    for src, width, dst, col in INPROJ_CHUNKS:
        out_refs[dst][:, col:col + width] = _dot(h, w_ref[:, src:src + width])


def _inproj(x2d, g, w_bf16):
    n, d = x2d.shape
    tm = min(ROW_TILE, n)
    widths = (A_WIDTH, KV_WIDTH, B_WIDTH)
    return pl.pallas_call(
        _inproj_kernel,
        grid=(n // tm,),
        in_specs=[
            pl.BlockSpec((tm, d), lambda i: (i, 0)),
            pl.BlockSpec((1, d), lambda i: (0, 0)),
            pl.BlockSpec((d, IN_WIDTH), lambda i: (0, 0)),
        ],
        out_specs=[pl.BlockSpec((tm, w), lambda i: (i, 0)) for w in widths],
        out_shape=[jax.ShapeDtypeStruct((n, w), F32) for w in widths],
        compiler_params=pltpu.CompilerParams(
            dimension_semantics=("arbitrary",), vmem_limit_bytes=VMEM_LIMIT),
        name="inproj",
    )(x2d, g, w_bf16)


def _outproj_kernel(or_ref, os_ref, oc_ref, x_ref, w_ref, g_ref, o_ref):
    mix = jnp.concatenate([or_ref[...], os_ref[...], oc_ref[...]], axis=-1)
    y = _dot(mix, w_ref[...])
    ms = jnp.mean(y * y, axis=-1, keepdims=True)
    o_ref[...] = x_ref[...] + y * lax.rsqrt(ms + EPS) * g_ref[...]


def _outproj(o_r, o_s, o_c, x2d, w_bf16, g):
    n, d = x2d.shape
    tm = min(ROW_TILE, n)
    row = lambda i: (i, 0)
    fixed = lambda i: (0, 0)
    return pl.pallas_call(
        _outproj_kernel,
        grid=(n // tm,),
        in_specs=[
            pl.BlockSpec((tm, RET_WIDTH), row),
            pl.BlockSpec((tm, SB_WIDTH), row),
            pl.BlockSpec((tm, CONV_WIDTH), row),
            pl.BlockSpec((tm, d), row),
            pl.BlockSpec(w_bf16.shape, fixed),
            pl.BlockSpec((1, d), fixed),
        ],
        out_specs=pl.BlockSpec((tm, d), row),
        out_shape=jax.ShapeDtypeStruct((n, d), F32),
        compiler_params=pltpu.CompilerParams(
            dimension_semantics=("arbitrary",), vmem_limit_bytes=VMEM_LIMIT),
        name="outproj",
    )(o_r, o_s, o_c, x2d, w_bf16, g)


def _rope_tables(pos):
    half = HEAD_DIM // 2
    inv = jnp.power(ROPE_BASE, -2.0 * jnp.arange(half, dtype=F32) / HEAD_DIM)
    ang = pos.astype(F32)[:, None] * inv[None, :]
    cos, sin = jnp.cos(ang), jnp.sin(ang)
    cos_t = jnp.concatenate([cos, cos, cos, cos], axis=-1)
    sin_t = jnp.concatenate([-sin, sin, -sin, sin], axis=-1)
    return cos_t, sin_t


def _retention_tables(c_valid, rows, keys):
    h = np.arange(RET_HEADS, dtype=np.float64)
    lg = np.log(1.0 - np.power(2.0, -5.0 - h))
    c = np.arange(rows, dtype=np.float64)
    s = np.arange(keys, dtype=np.float64)
    diff = c[:, None] - s[None, :]
    ok = (diff >= 0) & (s[None, :] < c_valid) & (c[:, None] < c_valid)
    dmat = np.where(ok[None], np.exp(np.maximum(diff, 0.0)[None] * lg[:, None, None]), 0.0)
    q_dec = np.exp((c[:, None] + 1.0) * lg[None, :])
    k_dec = np.where(c[:, None] < c_valid, np.exp((c_valid - 1.0 - c)[:, None] * lg[None, :]), 0.0)
    c_dec = np.exp(c_valid * lg)[None, :]
    lanes = lambda t: np.repeat(t, HEAD_DIM, axis=-1)
    return (jnp.asarray(dmat, F32), jnp.asarray(lanes(q_dec), F32),
            jnp.asarray(lanes(k_dec), F32), jnp.asarray(lanes(c_dec), F32))


def _retention_kernel(rq_ref, rk_ref, rv_ref, rg_ref, cos_ref, sin_ref, d_ref, qdec_ref,
                      kdec_ref, cdec_ref, gn_ref, s0_ref, o_ref, s_out_ref, s_scr, *, key_rows):
    c_idx = pl.program_id(1)

    @pl.when(c_idx == 0)
    def _():
        s_scr[...] = s0_ref[...]

    nb, rows = rq_ref.shape[0], rq_ref.shape[1]
    lane = lax.broadcasted_iota(jnp.int32, (rows, PAIR), 1)
    first = lane < HEAD_DIM
    low_half = (lane % HEAD_DIM) < (HEAD_DIM // 2)
    bd_r = lax.broadcasted_iota(jnp.int32, (PAIR, PAIR), 0) < HEAD_DIM
    bd_c = lax.broadcasted_iota(jnp.int32, (PAIR, PAIR), 1) < HEAD_DIM
    block_diag = bd_r == bd_c
    cos_t = cos_ref[...]
    sin_t = sin_ref[...]

    def rope(x):
        swapped = jnp.where(low_half, pltpu.roll(x, PAIR - HEAD_DIM // 2, 1),
                            pltpu.roll(x, HEAD_DIM // 2, 1))
        return x * cos_t + swapped * sin_t

    def pad_keys(x):
        if key_rows == rows:
            return x
        return jnp.concatenate([x, jnp.zeros((key_rows - rows, PAIR), x.dtype)], axis=0)

    for i, p in [(i, p) for i in range(nb) for p in range(RET_HEADS // 2)]:
        sl = slice(PAIR * p, PAIR * (p + 1))
        q = rope(rq_ref[i, :, sl])
        k = rope(rk_ref[i, :, sl]) * (HEAD_DIM ** -0.5)
        qb = q.astype(BF16)
        kb = pad_keys(k.astype(BF16))
        vb = pad_keys(rv_ref[i, :, sl].astype(BF16))
        zero = jnp.zeros_like(qb)
        att0 = _dot_nt(jnp.where(first, qb, zero), kb) * d_ref[2 * p]
        att1 = _dot_nt(jnp.where(first, zero, qb), kb) * d_ref[2 * p + 1]
        s_old = s_scr[i, p]
        o = jnp.where(first, _dot(att0.astype(BF16), vb), _dot(att1.astype(BF16), vb))
        o = o + _dot((q * qdec_ref[:, sl]).astype(BF16), s_old.astype(BF16))
        kd = pad_keys((k * kdec_ref[:, sl]).astype(BF16))
        kv = _dot_tn(kd, vb)
        s_scr[i, p] = s_old * cdec_ref[:, sl] + jnp.where(block_diag, kv, 0.0)

        inv_d = 1.0 / HEAD_DIM
        mu0 = jnp.sum(jnp.where(first, o, 0.0), axis=-1, keepdims=True) * inv_d
        mu1 = jnp.sum(jnp.where(first, 0.0, o), axis=-1, keepdims=True) * inv_d
        dlt = o - jnp.where(first, mu0, mu1)
        sq = dlt * dlt
        v0 = jnp.sum(jnp.where(first, sq, 0.0), axis=-1, keepdims=True) * inv_d
        v1 = jnp.sum(jnp.where(first, 0.0, sq), axis=-1, keepdims=True) * inv_d
        y = dlt * lax.rsqrt(jnp.where(first, v0, v1) + EPS) * gn_ref[:, sl]
        o_ref[i, :, sl] = (y * _silu(rg_ref[i, :, sl])).astype(o_ref.dtype)

    @pl.when(c_idx == pl.num_programs(1) - 1)
    def _():
        s_out_ref[...] = s_scr[...]


def _retention(p3, cos_t, sin_t, tables, gn_w, s0_bd, *, chunk, key_rows, nb):
    b, t, _ = p3.shape
    nc = t // chunk
    nb = min(nb, b)
    dmat, q_dec, k_dec, c_dec = tables
    col = lambda j: pl.BlockSpec((nb, chunk, RET_WIDTH), lambda i, c, j=j: (i, c, j))
    fixed2 = lambda a: pl.BlockSpec(a.shape, lambda i, c: (0, 0))
    state_spec = pl.BlockSpec((nb, RET_HEADS // 2, PAIR, PAIR), lambda i, c: (i, 0, 0, 0))
    return pl.pallas_call(
        functools.partial(_retention_kernel, key_rows=key_rows),
        grid=(b // nb, nc),
        in_specs=[
            col(0), col(1), col(2), col(3),
            pl.BlockSpec((chunk, PAIR), lambda i, c: (c, 0)),
            pl.BlockSpec((chunk, PAIR), lambda i, c: (c, 0)),
            pl.BlockSpec(dmat.shape, lambda i, c: (0, 0, 0)),
            fixed2(q_dec), fixed2(k_dec), fixed2(c_dec), fixed2(gn_w),
            state_spec,
        ],
        out_specs=[
            pl.BlockSpec((nb, chunk, RET_WIDTH), lambda i, c: (i, c, 0)),
            state_spec,
        ],
        out_shape=[
            jax.ShapeDtypeStruct((b, t, RET_WIDTH), BF16),
            jax.ShapeDtypeStruct(s0_bd.shape, F32),
        ],
        scratch_shapes=[pltpu.VMEM((nb, RET_HEADS // 2, PAIR, PAIR), F32)],
        compiler_params=pltpu.CompilerParams(
            dimension_semantics=("arbitrary", "arbitrary"), vmem_limit_bytes=VMEM_LIMIT),
        name="retention",
    )(p3, p3, p3, p3, cos_t, sin_t, dmat, q_dec, k_dec, c_dec, gn_w, s0_bd)


def _to_block_diag(s):
    b = s.shape[0]
    s = s.reshape(b, RET_HEADS // 2, 2, HEAD_DIM, HEAD_DIM)
    z = jnp.zeros_like(s[:, :, 0])
    top = jnp.concatenate([s[:, :, 0], z], axis=-1)
    bot = jnp.concatenate([z, s[:, :, 1]], axis=-1)
    return jnp.concatenate([top, bot], axis=-2)


def _from_block_diag(s_bd):
    b = s_bd.shape[0]
    a = s_bd[:, :, :HEAD_DIM, :HEAD_DIM]
    d = s_bd[:, :, HEAD_DIM:, HEAD_DIM:]
    return jnp.stack([a, d], axis=2).reshape(b, RET_HEADS, HEAD_DIM, HEAD_DIM)


def _softplus_parts(z):
    u = jnp.exp2(jnp.minimum(z * LOG2_E, 126.0))
    sp = jnp.maximum(jnp.log(1.0 + u), z)
    return sp, z - sp


def _later_sum(sp, upper):
    hi, lo = _split_hi_lo(sp)
    return _dot(jnp.concatenate([hi, lo], axis=1), jnp.concatenate([upper, upper], axis=0))


def _strict_upper(n):
    r = lax.broadcasted_iota(jnp.int32, (n, n), 0)
    c = lax.broadcasted_iota(jnp.int32, (n, n), 1)
    return jnp.where(r > c, 1.0, 0.0).astype(BF16)


def _sb_prompt_kernel(bias_ref, q_ref, k_ref, v_ref, g_ref, o_ref, kb_scr, vb_scr, acc_scr, car_scr,
                      za_scr, zb_scr, w_scr):
    hp = pl.program_id(1)
    qi = pl.program_id(2)
    tq, tk = SB_TQ, SB_TK
    ratio = tq // tk

    @pl.when(qi == 0)
    def _():
        kb_scr[...] = k_ref[...].astype(BF16)
        vb_scr[...] = v_ref[...].astype(BF16)

    q = q_ref[...] * (HEAD_DIM ** -0.5)
    lane = lax.broadcasted_iota(jnp.int32, (tq, PAIR), 1)
    first = lane < HEAD_DIM
    zero = jnp.zeros((tq, PAIR), F32)
    q2 = jnp.concatenate([jnp.where(first, q, zero), jnp.where(first, zero, q)], axis=0).astype(BF16)
    row2 = lax.broadcasted_iota(jnp.int32, (2 * tq, 1), 0)
    bias2 = jnp.where(row2 < tq, bias_ref[2 * hp], bias_ref[2 * hp + 1])
    upper = _strict_upper(tk)
    col = lax.broadcasted_iota(jnp.int32, (2 * tq, tk), 1)
    row = lax.broadcasted_iota(jnp.int32, (2 * tq, tk), 0) % tq
    acc_scr[...] = jnp.zeros_like(acc_scr)
    car_scr[...] = jnp.zeros_like(car_scr)

    def rows_of(scr, j):
        if isinstance(j, int):
            return scr[j * tk:(j + 1) * tk, :]
        return scr[pl.ds(pl.multiple_of(j * tk, tk), tk), :]

    def scores(tiles):
        return [_dot_nt(q2, rows_of(kb_scr, j)) for j in tiles]

    def weights(zs, masks):
        carry = car_scr[...]
        sps, lbs, carries = [], [], []
        for z, mask in zip(zs, masks):
            sp, lb = _softplus_parts(z + bias2)
            if mask is not None:
                sp = jnp.where(mask, sp, 0.0)
            carries.append(carry)
            carry = carry + jnp.sum(sp, axis=-1, keepdims=True)
            sps.append(sp)
            lbs.append(lb)
        car_scr[...] = carry
        later = [_dot(sp.astype(BF16), upper) for sp in sps]
        out = []
        for lb, lt, cr, mask in zip(lbs, later, carries, masks):
            a = jnp.exp(lb - lt - cr)
            if mask is not None:
                a = jnp.where(mask, a, 0.0)
            out.append(a.astype(BF16))
        return out

    def accumulate(ws, tiles):
        acc = acc_scr[...]
        for w, j in zip(ws, tiles):
            acc = acc + _dot(w, rows_of(vb_scr, j))
        acc_scr[...] = acc

    n_full = ratio * qi
    odd = n_full % 2
    n_pairs = n_full // 2
    top = n_full - odd - 1
    z_scrs = (za_scr, zb_scr)
    for z_ref, z in zip(z_scrs, scores([jnp.maximum(top, 0), jnp.maximum(top - 1, 0)])):
        z_ref[0] = z
    w_scr[1] = jnp.zeros(w_scr.shape[1:], BF16)

    diag = [ratio * qi + d for d in reversed(range(ratio))]
    masks = [(col + d * tk) < row for d in reversed(range(ratio))]
    accumulate(weights(scores(diag), masks), diag)

    @pl.when(odd == 1)
    def _():
        accumulate(weights(scores([n_full - 1]), [None]), [n_full - 1])

    chunk = 2 * tq // SB_SPLIT

    def body(i, carry):
        slot = i % 2
        j = top - 2 * i
        prev = [jnp.minimum(j + 2, top), jnp.minimum(j + 1, top)]
        nxt = [jnp.maximum(j - 2, 0), jnp.maximum(j - 3, 0)]
        for h in range(SB_SPLIT):
            rs = slice(h * chunk, (h + 1) * chunk)
            acc = acc_scr[rs, :]
            for t, jp in enumerate(prev):
                acc = acc + _dot(w_scr[1 - slot, t, rs, :], rows_of(vb_scr, jp))
            acc_scr[rs, :] = acc
        args = []
        for h in range(SB_SPLIT):
            rs = slice(h * chunk, (h + 1) * chunk)
            row_carry = car_scr[rs, :]
            for t, (z_ref, j_next) in enumerate(zip(z_scrs, nxt)):
                sp, lb = _softplus_parts(z_ref[slot, rs, :] + bias2[rs])
                later = _dot(sp.astype(BF16), upper)
                z_ref[1 - slot, rs, :] = _dot_nt(q2[rs], rows_of(kb_scr, j_next))
                args.append((t, rs, lb, later, row_carry))
                row_carry = row_carry + jnp.sum(sp, axis=-1, keepdims=True)
            car_scr[rs, :] = row_carry
        for t, rs, lb, later, cr in args:
            w_scr[slot, t, rs, :] = jnp.exp(lb - later - cr).astype(BF16)
        return carry

    lax.fori_loop(0, n_pairs, body, 0)

    @pl.when(n_pairs > 0)
    def _():
        last = (n_pairs - 1) % 2
        accumulate([w_scr[last, 0], w_scr[last, 1]], [1, 0])

    acc = acc_scr[...]
    o = jnp.where(first, acc[:tq], acc[tq:])
    o_ref[...] = (o * _silu(g_ref[...])).astype(o_ref.dtype)


def _sb_prompt(pa3, kv3, pb3, bias):
    b, t, _ = pa3.shape
    assert SB_TQ % SB_TK == 0 and t % SB_TQ == 0
    nq = t // SB_TQ
    pairs = SB_HEADS // 2
    qcol = lambda off: pl.BlockSpec(
        (None, SB_TQ, PAIR), lambda i, hp, qi, off=off: (i, qi, off // PAIR + hp))
    kcol = lambda off: pl.BlockSpec(
        (None, t, PAIR), lambda i, hp, qi, off=off: (i, 0, off // PAIR + hp))
    return pl.pallas_call(
        _sb_prompt_kernel,
        grid=(b, pairs, nq),
        in_specs=[
            pl.BlockSpec(memory_space=pltpu.SMEM),
            qcol(A_SQ), kcol(KV_SK), kcol(KV_SV), qcol(B_SG),
        ],
        out_specs=pl.BlockSpec((None, SB_TQ, PAIR), lambda i, hp, qi: (i, qi, hp)),
        out_shape=jax.ShapeDtypeStruct((b, t, SB_WIDTH), BF16),
        scratch_shapes=[
            pltpu.VMEM((t, PAIR), BF16), pltpu.VMEM((t, PAIR), BF16),
            pltpu.VMEM((2 * SB_TQ, PAIR), F32), pltpu.VMEM((2 * SB_TQ, 1), F32),
            pltpu.VMEM((2, 2 * SB_TQ, SB_TK), F32), pltpu.VMEM((2, 2 * SB_TQ, SB_TK), F32),
            pltpu.VMEM((2, 2, 2 * SB_TQ, SB_TK), BF16),
        ],
        compiler_params=pltpu.CompilerParams(
            dimension_semantics=("arbitrary", "arbitrary", "arbitrary"),
            vmem_limit_bytes=VMEM_LIMIT),
        name="sb_prompt",
    )(bias, pa3, kv3, kv3, pb3)


def _sb_sample_kernel(pt_ref, bias_ref, q_ref, kn_ref, vn_ref, g_ref, *rest, n_pages, page):
    k_refs = rest[:n_pages]
    v_refs = rest[n_pages:2 * n_pages]
    o_ref = rest[2 * n_pages]
    rows = SAMPLE_PAD
    nrow = SB_HEADS * rows
    q = q_ref[...] * (HEAD_DIM ** -0.5)
    lane_head = lax.broadcasted_iota(jnp.int32, (rows, SB_WIDTH), 1) // HEAD_DIM
    zero = jnp.zeros((rows, SB_WIDTH), F32)
    qbd = jnp.concatenate(
        [jnp.where(lane_head == h, q, zero) for h in range(SB_HEADS)], axis=0).astype(BF16)
    row = lax.broadcasted_iota(jnp.int32, (nrow, page), 0)
    col = lax.broadcasted_iota(jnp.int32, (nrow, page), 1)
    bias = jnp.zeros((nrow, page), F32)
    for h in range(SB_HEADS):
        bias = jnp.where(row // rows == h, bias_ref[h], bias)
    causal = col < (row % rows)
    upper = _strict_upper(page)
    pad = jnp.zeros((page - rows, SB_WIDTH), BF16)

    new_k = jnp.concatenate([kn_ref[...].astype(BF16), pad], axis=0)
    new_v = jnp.concatenate([vn_ref[...].astype(BF16), pad], axis=0)
    zs = [_dot_nt(qbd, new_k)]
    for j in reversed(range(n_pages)):
        zs.append(_dot(qbd, k_refs[j][...].astype(BF16)))

    sps, lbs = [], []
    for i, z in enumerate(zs):
        sp, lb = _softplus_parts(z + bias)
        if i == 0:
            sp = jnp.where(causal, sp, 0.0)
        sps.append(sp)
        lbs.append(lb)
    halves = []
    for sp in sps:
        halves.extend(_split_hi_lo(sp))
    later = _dot(jnp.concatenate(halves, axis=0), upper)
    carry = jnp.zeros((nrow, 1), F32)
    acc = jnp.zeros((nrow, SB_WIDTH), F32)
    for i, (sp, lb) in enumerate(zip(sps, lbs)):
        within = later[2 * i * nrow:(2 * i + 1) * nrow] + later[(2 * i + 1) * nrow:(2 * i + 2) * nrow]
        a = jnp.exp(lb - within - carry)
        if i == 0:
            acc = acc + _dot(jnp.where(causal, a, 0.0).astype(BF16), new_v)
        else:
            acc = acc + _dot_nt(a.astype(BF16), v_refs[n_pages - i][...].astype(BF16))
        carry = carry + jnp.sum(sp, axis=-1, keepdims=True)
    out = jnp.zeros((rows, SB_WIDTH), F32)
    for h in range(SB_HEADS):
        out = out + jnp.where(lane_head == h, acc[h * rows:(h + 1) * rows], 0.0)
    o_ref[...] = (out * _silu(g_ref[...])).astype(o_ref.dtype)


def _sb_sample(pa3, kv3, pb3, cache_k, cache_v, page_table, bias, layer):
    b = pa3.shape[0]
    n_pages = page_table.shape[1]
    page = cache_k.shape[3]
    tok = lambda off: pl.BlockSpec(
        (None, SAMPLE_PAD, SB_WIDTH), lambda i, pt, off=off: (i, 0, off // SB_WIDTH))
    page_spec = lambda j: pl.BlockSpec(
        (None, None, SB_WIDTH, page), lambda i, pt, j=j: (layer, pt[i, j], 0, 0))
    grid_spec = pltpu.PrefetchScalarGridSpec(
        num_scalar_prefetch=1,
        grid=(b,),
        in_specs=[pl.BlockSpec(memory_space=pltpu.SMEM),
                  tok(A_SQ), tok(KV_SK), tok(KV_SV), tok(B_SG)]
        + [page_spec(j) for j in range(n_pages)] * 2,
        out_specs=pl.BlockSpec((None, SAMPLE_PAD, SB_WIDTH), lambda i, pt: (i, 0, 0)),
    )
    return pl.pallas_call(
        functools.partial(_sb_sample_kernel, n_pages=n_pages, page=page),
        grid_spec=grid_spec,
        out_shape=jax.ShapeDtypeStruct((b, SAMPLE_PAD, SB_WIDTH), BF16),
        compiler_params=pltpu.CompilerParams(
            dimension_semantics=("arbitrary",), vmem_limit_bytes=VMEM_LIMIT),
        name="sb_sample",
    )(page_table, bias, pa3, kv3, kv3, pb3, *([cache_k] * n_pages), *([cache_v] * n_pages))


def _conv_tail(y, lnw_ref, lnb_ref, pw_ref, gate):
    mu = jnp.mean(y, axis=-1, keepdims=True)
    d = y - mu
    var = jnp.mean(d * d, axis=-1, keepdims=True)
    z = d * lax.rsqrt(var + EPS) * lnw_ref[...] + lnb_ref[...]
    return _dot(_silu(z).astype(BF16), pw_ref[...]) * _silu(gate)


def _conv_prompt_kernel(ca_ref, cb_ref, cg_ref, dww_ref, dwb_ref, lnw_ref, lnb_ref, pw_ref,
                        o_ref, st_ref, ext_scr, phase_scr):
    ti = pl.program_id(1)
    tt = ca_ref.shape[0]

    @pl.when(ti == 0)
    def _():
        ext_scr[0:CONV_HALO, :] = jnp.zeros((CONV_HALO, CONV_WIDTH), F32)

    @pl.when(ti > 0)
    def _():
        ext_scr[0:CONV_HALO, :] = ext_scr[tt:tt + CONV_HALO, :]

    u = ca_ref[...] * _sigmoid(cb_ref[...])
    ext_scr[CONV_HALO:CONV_HALO + tt, :] = u
    first_tap = CONV_HALO - (CONV_K - 1)
    for r in range(1, SUBLANES):
        phase_scr[r - 1] = ext_scr[r:r + phase_scr.shape[1], :]
    y = jnp.zeros((tt, CONV_WIDTH), F32) + dwb_ref[...]
    for j in range(CONV_K):
        r = (first_tap + j) % SUBLANES
        a = first_tap + j - r
        window = ext_scr[a:a + tt, :] if r == 0 else phase_scr[r - 1, a:a + tt, :]
        y = y + window * dww_ref[j:j + 1, :]
    o_ref[...] = _conv_tail(y, lnw_ref, lnb_ref, pw_ref, cg_ref[...]).astype(o_ref.dtype)

    @pl.when(ti == pl.num_programs(1) - 1)
    def _():
        st_ref[...] = ext_scr[CONV_HALO + tt - (CONV_K - 1):CONV_HALO + tt, :]


def _conv_prompt(p3, dw_w, dw_b, ln_w, ln_b, pw_bf16):
    b, t, _ = p3.shape
    tt = min(CONV_TT, t)
    col = lambda off: pl.BlockSpec(
        (None, tt, CONV_WIDTH), lambda i, c, off=off: (i, c, off // CONV_WIDTH))
    fixed = lambda a: pl.BlockSpec(a.shape, lambda i, c: (0, 0))
    return pl.pallas_call(
        _conv_prompt_kernel,
        grid=(b, t // tt),
        in_specs=[col(B_CA), col(B_CB), col(B_CG),
                  fixed(dw_w), fixed(dw_b), fixed(ln_w), fixed(ln_b), fixed(pw_bf16)],
        out_specs=[
            pl.BlockSpec((None, tt, CONV_WIDTH), lambda i, c: (i, c, 0)),
            pl.BlockSpec((None, CONV_K - 1, CONV_WIDTH), lambda i, c: (i, 0, 0)),
        ],
        out_shape=[
            jax.ShapeDtypeStruct((b, t, CONV_WIDTH), BF16),
            jax.ShapeDtypeStruct((b, CONV_K - 1, CONV_WIDTH), F32),
        ],
        scratch_shapes=[
            pltpu.VMEM((tt + CONV_HALO, CONV_WIDTH), F32),
            pltpu.VMEM((SUBLANES - 1, tt + CONV_HALO - SUBLANES, CONV_WIDTH), F32),
        ],
        compiler_params=pltpu.CompilerParams(
            dimension_semantics=("arbitrary", "arbitrary"), vmem_limit_bytes=VMEM_LIMIT),
        name="conv_prompt",
    )(p3, p3, p3, dw_w, dw_b, ln_w, ln_b, pw_bf16)


def _conv_sample_kernel(ca_ref, cb_ref, cg_ref, buf_ref, dww_ref, dwb_ref, lnw_ref, lnb_ref,
                        pw_ref, o_ref, st_ref, *, n_tok):
    hist = CONV_K - 1
    rows = [buf_ref[:, j, :] for j in range(hist)]
    rows += [ca_ref[:, t, :] * _sigmoid(cb_ref[:, t, :]) for t in range(n_tok)]
    for t in range(n_tok):
        y = jnp.zeros_like(rows[0]) + dwb_ref[...]
        for j in range(CONV_K):
            y = y + rows[t + j] * dww_ref[j:j + 1, :]
        o_ref[:, t, :] = _conv_tail(y, lnw_ref, lnb_ref, pw_ref, cg_ref[:, t, :]).astype(o_ref.dtype)
    for t in range(n_tok, SAMPLE_PAD):
        o_ref[:, t, :] = jnp.zeros(rows[0].shape, o_ref.dtype)
    for j in range(hist):
        st_ref[:, j, :] = rows[n_tok + j]


def _conv_sample(p3, buf, dw_w, dw_b, ln_w, ln_b, pw_bf16, n_tok):
    b = p3.shape[0]
    bb = min(32, b)
    col = lambda off: pl.BlockSpec(
        (bb, SAMPLE_PAD, CONV_WIDTH), lambda i, off=off: (i, 0, off // CONV_WIDTH))
    fixed = lambda a: pl.BlockSpec(a.shape, lambda i: (0, 0))
    st_spec = pl.BlockSpec((bb, CONV_K - 1, CONV_WIDTH), lambda i: (i, 0, 0))
    return pl.pallas_call(
        functools.partial(_conv_sample_kernel, n_tok=n_tok),
        grid=(b // bb,),
        in_specs=[col(B_CA), col(B_CB), col(B_CG), st_spec,
                  fixed(dw_w), fixed(dw_b), fixed(ln_w), fixed(ln_b), fixed(pw_bf16)],
        out_specs=[pl.BlockSpec((bb, SAMPLE_PAD, CONV_WIDTH), lambda i: (i, 0, 0)), st_spec],
        out_shape=[
            jax.ShapeDtypeStruct((b, SAMPLE_PAD, CONV_WIDTH), BF16),
            jax.ShapeDtypeStruct(buf.shape, F32),
        ],
        compiler_params=pltpu.CompilerParams(
            dimension_semantics=("arbitrary",), vmem_limit_bytes=VMEM_LIMIT),
        name="conv_sample",
    )(p3, p3, p3, buf, dw_w, dw_b, ln_w, ln_b, pw_bf16)


def kernel(x_prompt, x_sample, state_ret, cache_k, cache_v, state_conv, page_table, norm_pre, w_in,
           ret_gn_w, sb_bias, conv_dw_w, conv_dw_b, conv_ln_w, conv_ln_b, conv_pw_w, w_out, norm_post):
    bp, tp, d = x_prompt.shape
    bs, ts, _ = x_sample.shape
    depth = w_in.shape[0]
    n_pages = page_table.shape[1]
    page = cache_k.shape[2]
    past_len = n_pages * page
    assert ts <= SAMPLE_PAD and tp % RET_CHUNK == 0

    xp = x_prompt.reshape(bp * tp, d)
    xs = jnp.pad(x_sample, ((0, 0), (0, SAMPLE_PAD - ts), (0, 0))).reshape(bs * SAMPLE_PAD, d)
    ck = cache_k.transpose(0, 1, 3, 4, 2).reshape(depth, cache_k.shape[1], SB_WIDTH, page)
    cv = cache_v.transpose(0, 1, 3, 4, 2).reshape(depth, cache_v.shape[1], SB_WIDTH, page)

    cos_p, sin_p = _rope_tables(jnp.arange(tp))
    cos_s, sin_s = _rope_tables(past_len + jnp.arange(SAMPLE_PAD))
    tab_p = _retention_tables(RET_CHUNK, RET_CHUNK, RET_CHUNK)
    tab_s = _retention_tables(ts, SAMPLE_PAD, PAIR)
    s0_p = jnp.zeros((bp, RET_HEADS // 2, PAIR, PAIR), F32)

    outs = [[] for _ in range(8)]
    for l in range(depth):
        w_in_b = w_in[l].astype(BF16)
        w_out_b = w_out[l].astype(BF16)
        pw_b = conv_pw_w[l].astype(BF16)
        g_pre, g_post = norm_pre[l][None, :], norm_post[l][None, :]
        gn_w = ret_gn_w[l][None, :]
        dw_b, ln_w, ln_b = conv_dw_b[l][None, :], conv_ln_w[l][None, :], conv_ln_b[l][None, :]

        pa, kv_p, pb = [a.reshape(bp, tp, -1) for a in _inproj(xp, g_pre, w_in_b)]
        or_p, s_p = _retention(pa, cos_p, sin_p, tab_p, gn_w, s0_p, chunk=RET_CHUNK,
                               key_rows=RET_CHUNK, nb=RET_NB_PROMPT)
        os_p = _sb_prompt(pa, kv_p, pb, sb_bias[l])
        oc_p, cst_p = _conv_prompt(pb, conv_dw_w[l], dw_b, ln_w, ln_b, pw_b)
        xp = _outproj(or_p.reshape(bp * tp, -1), os_p.reshape(bp * tp, -1), oc_p.reshape(bp * tp, -1),
                      xp, w_out_b, g_post)

        sa, kv_s, sb = [a.reshape(bs, SAMPLE_PAD, -1) for a in _inproj(xs, g_pre, w_in_b)]
        or_s, s_s = _retention(sa, cos_s, sin_s, tab_s, gn_w, _to_block_diag(state_ret[l]),
                               chunk=SAMPLE_PAD, key_rows=PAIR, nb=RET_NB_SAMPLE)
        os_s = _sb_sample(sa, kv_s, sb, ck, cv, page_table, sb_bias[l], l)
        oc_s, cst_s = _conv_sample(sb, state_conv[l], conv_dw_w[l], dw_b, ln_w, ln_b, pw_b, ts)
        n_s = bs * SAMPLE_PAD
        xs = _outproj(or_s.reshape(n_s, -1), os_s.reshape(n_s, -1), oc_s.reshape(n_s, -1),
                      xs, w_out_b, g_post)

        outs[0].append(_from_block_diag(s_p))
        outs[1].append(_from_block_diag(s_s))
        outs[2].append(kv_p[:, :, KV_SK:KV_SK + SB_WIDTH].reshape(bp, tp, SB_HEADS, HEAD_DIM))
        outs[3].append(kv_p[:, :, KV_SV:KV_SV + SB_WIDTH].reshape(bp, tp, SB_HEADS, HEAD_DIM))
        outs[4].append(kv_s[:, :ts, KV_SK:KV_SK + SB_WIDTH].reshape(bs, ts, SB_HEADS, HEAD_DIM))
        outs[5].append(kv_s[:, :ts, KV_SV:KV_SV + SB_WIDTH].reshape(bs, ts, SB_HEADS, HEAD_DIM))
        outs[6].append(cst_p)
        outs[7].append(cst_s)

    y_prompt = xp.reshape(bp, tp, d)
    y_sample = xs.reshape(bs, SAMPLE_PAD, d)[:, :ts]
    return (y_prompt, y_sample) + tuple(jnp.stack(o) for o in outs)
```

```python
import functools

import numpy as np
import jax
import jax.numpy as jnp
from jax import lax
from jax.experimental import pallas as pl
from jax.experimental.pallas import tpu as pltpu

F32 = jnp.float32
BF16 = jnp.bfloat16

HEAD_DIM = 64
PAIR = 2 * HEAD_DIM
RET_HEADS = 6
SB_HEADS = 6
RET_WIDTH = RET_HEADS * HEAD_DIM
SB_WIDTH = SB_HEADS * HEAD_DIM
CONV_WIDTH = 256
CONV_K = 31
IN_WIDTH = 4 * RET_WIDTH + 4 * SB_WIDTH + 3 * CONV_WIDTH
RET_CHUNK = 128
ROPE_BASE = 10000.0
EPS = 1e-6
LOG2_E = 1.4426950408889634
SUBLANES = 8
SAMPLE_PAD = SUBLANES

INPROJ_WIDTHS = (1920, SB_WIDTH, SB_WIDTH, 1152)
A_SQ = 1536
B_CA, B_CB, B_CG, B_SG = 0, 256, 512, 768
INPROJ_CHUNKS = ((0, 768, 0, 0), (768, 768, 0, 768), (1536, 384, 0, 1536), (1920, 384, 1, 0),
                 (2304, 384, 2, 0), (3072, 768, 3, 0), (2688, 384, 3, 768))

ROW_TILE = 512
SB_TQ = 512
SB_TK = 256
RET_NB_PROMPT = 2
RET_NB_SAMPLE = 8
SB_NB_SAMPLE = 1
CONV_TT = 512
CONV_HALO = 32
VMEM_LIMIT = 56 * 1024 * 1024


def _dot(a, b):
    return jnp.dot(a, b, preferred_element_type=F32)


def _dot_nt(a, b):
    return lax.dot_general(a, b, (((1,), (1,)), ((), ())), preferred_element_type=F32)


def _dot_tn(a, b):
    return lax.dot_general(a, b, (((0,), (0,)), ((), ())), preferred_element_type=F32)


def _sigmoid(x):
    return 1.0 / (1.0 + jnp.exp(-x))


def _silu(x):
    return x * _sigmoid(x)


def _split_hi_lo(x):
    hi = x.astype(BF16)
    lo = (x - hi.astype(F32)).astype(BF16)
    return hi, lo


def _inproj_kernel(x_ref, g_ref, w_ref, *out_refs):
    x = x_ref[...]
    ms = jnp.mean(x * x, axis=-1, keepdims=True)
    h = (x * lax.rsqrt(ms + EPS) * g_ref[...]).astype(BF16)
    for src, width, dst, col in INPROJ_CHUNKS:
        out_refs[dst][:, col:col + width] = _dot(h, w_ref[:, src:src + width])


def _inproj(x2d, g, w_bf16):
    n, d = x2d.shape
    tm = min(ROW_TILE, n)
    widths = INPROJ_WIDTHS
    return pl.pallas_call(
        _inproj_kernel,
        grid=(n // tm,),
        in_specs=[
            pl.BlockSpec((tm, d), lambda i: (i, 0)),
            pl.BlockSpec((1, d), lambda i: (0, 0)),
            pl.BlockSpec((d, IN_WIDTH), lambda i: (0, 0)),
        ],
        out_specs=[pl.BlockSpec((tm, w), lambda i: (i, 0)) for w in widths],
        out_shape=[jax.ShapeDtypeStruct((n, w), F32) for w in widths],
        compiler_params=pltpu.CompilerParams(
            dimension_semantics=("arbitrary",), vmem_limit_bytes=VMEM_LIMIT),
        name="inproj",
    )(x2d, g, w_bf16)


def _inproj_prompt_kernel(x_ref, g_ref, w_ref, wkt_ref, wvt_ref, kt_in, vt_in,
                          a_ref, b_ref, kt_ref, vt_ref):
    del kt_in, vt_in
    x = x_ref[...]
    ms = jnp.mean(x * x, axis=-1, keepdims=True)
    h = (x * lax.rsqrt(ms + EPS) * g_ref[...]).astype(BF16)
    for src, width, dst, col in INPROJ_CHUNKS:
        if dst == 0:
            a_ref[:, col:col + width] = _dot(h, w_ref[:, src:src + width])
        elif dst == 3:
            b_ref[:, col:col + width] = _dot(h, w_ref[:, src:src + width])
    kt_ref[...] = _dot_nt(wkt_ref[...], h)
    vt_ref[...] = _dot_nt(wvt_ref[...], h)


def _inproj_prompt(x2d, g, w_bf16, kt_buf, vt_buf, layer):
    n, d = x2d.shape
    t = kt_buf.shape[3]
    tm = min(ROW_TILE, t)
    per_seq = t // tm
    off_k, off_v = INPROJ_CHUNKS[3][0], INPROJ_CHUNKS[4][0]
    wkt = w_bf16[:, off_k:off_k + SB_WIDTH].T
    wvt = w_bf16[:, off_v:off_v + SB_WIDTH].T
    row = lambda i: (i, 0)
    fixed = lambda i: (0, 0)
    t_spec = pl.BlockSpec((None, None, SB_WIDTH, tm), lambda i: (layer, i // per_seq, 0, i % per_seq))
    return pl.pallas_call(
        _inproj_prompt_kernel,
        grid=(n // tm,),
        in_specs=[
            pl.BlockSpec((tm, d), row),
            pl.BlockSpec((1, d), fixed),
            pl.BlockSpec((d, IN_WIDTH), fixed),
            pl.BlockSpec(wkt.shape, fixed),
            pl.BlockSpec(wvt.shape, fixed),
            pl.BlockSpec(memory_space=pl.ANY),
            pl.BlockSpec(memory_space=pl.ANY),
        ],
        out_specs=[pl.BlockSpec((tm, INPROJ_WIDTHS[0]), row), pl.BlockSpec((tm, INPROJ_WIDTHS[3]), row),
                   t_spec, t_spec],
        out_shape=[jax.ShapeDtypeStruct((n, INPROJ_WIDTHS[0]), F32),
                   jax.ShapeDtypeStruct((n, INPROJ_WIDTHS[3]), F32),
                   jax.ShapeDtypeStruct(kt_buf.shape, F32), jax.ShapeDtypeStruct(vt_buf.shape, F32)],
        input_output_aliases={5: 2, 6: 3},
        compiler_params=pltpu.CompilerParams(
            dimension_semantics=("arbitrary",), vmem_limit_bytes=VMEM_LIMIT),
        name="inproj_prompt",
    )(x2d, g, w_bf16, wkt, wvt, kt_buf, vt_buf)


def _outproj_kernel(or_ref, os_ref, oc_ref, x_ref, w_ref, g_ref, o_ref):
    mix = jnp.concatenate([or_ref[...], os_ref[...], oc_ref[...]], axis=-1)
    y = _dot(mix, w_ref[...])
    ms = jnp.mean(y * y, axis=-1, keepdims=True)
    o_ref[...] = x_ref[...] + y * lax.rsqrt(ms + EPS) * g_ref[...]


def _outproj(o_r, o_s, o_c, x2d, w_bf16, g):
    n, d = x2d.shape
    tm = min(ROW_TILE, n)
    row = lambda i: (i, 0)
    fixed = lambda i: (0, 0)
    return pl.pallas_call(
        _outproj_kernel,
        grid=(n // tm,),
        in_specs=[
            pl.BlockSpec((tm, RET_WIDTH), row),
            pl.BlockSpec((tm, SB_WIDTH), row),
            pl.BlockSpec((tm, CONV_WIDTH), row),
            pl.BlockSpec((tm, d), row),
            pl.BlockSpec(w_bf16.shape, fixed),
            pl.BlockSpec((1, d), fixed),
        ],
        out_specs=pl.BlockSpec((tm, d), row),
        out_shape=jax.ShapeDtypeStruct((n, d), F32),
        compiler_params=pltpu.CompilerParams(
            dimension_semantics=("arbitrary",), vmem_limit_bytes=VMEM_LIMIT),
        name="outproj",
    )(o_r, o_s, o_c, x2d, w_bf16, g)


def _rope_tables(pos):
    half = HEAD_DIM // 2
    inv = jnp.power(ROPE_BASE, -2.0 * jnp.arange(half, dtype=F32) / HEAD_DIM)
    ang = pos.astype(F32)[:, None] * inv[None, :]
    cos, sin = jnp.cos(ang), jnp.sin(ang)
    cos_t = jnp.concatenate([cos, cos, cos, cos], axis=-1)
    sin_t = jnp.concatenate([-sin, sin, -sin, sin], axis=-1)
    return cos_t, sin_t


def _retention_tables(c_valid, rows, keys):
    h = np.arange(RET_HEADS, dtype=np.float64)
    lg = np.log(1.0 - np.power(2.0, -5.0 - h))
    c = np.arange(rows, dtype=np.float64)
    s = np.arange(keys, dtype=np.float64)
    diff = c[:, None] - s[None, :]
    ok = (diff >= 0) & (s[None, :] < c_valid) & (c[:, None] < c_valid)
    dmat = np.where(ok[None], np.exp(np.maximum(diff, 0.0)[None] * lg[:, None, None]), 0.0)
    q_dec = np.exp((c[:, None] + 1.0) * lg[None, :])
    k_dec = np.where(c[:, None] < c_valid, np.exp((c_valid - 1.0 - c)[:, None] * lg[None, :]), 0.0)
    c_dec = np.exp(c_valid * lg)[None, :]
    lanes = lambda t: np.repeat(t, HEAD_DIM, axis=-1)
    return (jnp.asarray(dmat, F32), jnp.asarray(lanes(q_dec), F32),
            jnp.asarray(lanes(k_dec), F32), jnp.asarray(lanes(c_dec), F32))


def _retention_kernel(rq_ref, rk_ref, rv_ref, rg_ref, cos_ref, sin_ref, d_ref, qdec_ref,
                      kdec_ref, cdec_ref, gn_ref, s0_ref, o_ref, s_out_ref, s_scr, *, key_rows):
    c_idx = pl.program_id(1)
    nb, rows = rq_ref.shape[0], rq_ref.shape[1]
    lane = lax.broadcasted_iota(jnp.int32, (rows, PAIR), 1)
    first = lane < HEAD_DIM
    low_half = (lane % HEAD_DIM) < (HEAD_DIM // 2)
    bd_r = lax.broadcasted_iota(jnp.int32, (PAIR, PAIR), 0) < HEAD_DIM
    bd_c = lax.broadcasted_iota(jnp.int32, (PAIR, PAIR), 1) < HEAD_DIM
    block_diag = bd_r == bd_c

    @pl.when(c_idx == 0)
    def _():
        for i, p in [(i, p) for i in range(nb) for p in range(RET_HEADS // 2)]:
            s_scr[i, p] = jnp.where(block_diag, s0_ref[i, p], 0.0)
    cos_t = cos_ref[...]
    sin_t = sin_ref[...]

    def rope(x):
        swapped = jnp.where(low_half, pltpu.roll(x, PAIR - HEAD_DIM // 2, 1),
                            pltpu.roll(x, HEAD_DIM // 2, 1))
        return x * cos_t + swapped * sin_t

    def pad_keys(x):
        if key_rows == rows:
            return x
        return jnp.concatenate([x, jnp.zeros((key_rows - rows, PAIR), x.dtype)], axis=0)

    for i, p in [(i, p) for i in range(nb) for p in range(RET_HEADS // 2)]:
        sl = slice(PAIR * p, PAIR * (p + 1))
        q = rope(rq_ref[i, :, sl])
        k = rope(rk_ref[i, :, sl]) * (HEAD_DIM ** -0.5)
        qb = q.astype(BF16)
        kb = pad_keys(k.astype(BF16))
        vb = pad_keys(rv_ref[i, :, sl].astype(BF16))
        zero = jnp.zeros_like(qb)
        att0 = _dot_nt(jnp.where(first, qb, zero), kb) * d_ref[2 * p]
        att1 = _dot_nt(jnp.where(first, zero, qb), kb) * d_ref[2 * p + 1]
        s_old = s_scr[i, p]
        o = jnp.where(first, _dot(att0.astype(BF16), vb), _dot(att1.astype(BF16), vb))
        o = o + _dot((q * qdec_ref[:, sl]).astype(BF16), s_old.astype(BF16))
        kd = pad_keys((k * kdec_ref[:, sl]).astype(BF16))
        kv = _dot_tn(kd, vb)
        s_scr[i, p] = s_old * cdec_ref[:, sl] + jnp.where(block_diag, kv, 0.0)

        inv_d = 1.0 / HEAD_DIM
        mu0 = jnp.sum(jnp.where(first, o, 0.0), axis=-1, keepdims=True) * inv_d
        mu1 = jnp.sum(jnp.where(first, 0.0, o), axis=-1, keepdims=True) * inv_d
        dlt = o - jnp.where(first, mu0, mu1)
        sq = dlt * dlt
        v0 = jnp.sum(jnp.where(first, sq, 0.0), axis=-1, keepdims=True) * inv_d
        v1 = jnp.sum(jnp.where(first, 0.0, sq), axis=-1, keepdims=True) * inv_d
        y = dlt * lax.rsqrt(jnp.where(first, v0, v1) + EPS) * gn_ref[:, sl]
        o_ref[i, :, sl] = (y * _silu(rg_ref[i, :, sl])).astype(o_ref.dtype)

    @pl.when(c_idx == pl.num_programs(1) - 1)
    def _():
        for i, p in [(i, p) for i in range(nb) for p in range(RET_HEADS // 2)]:
            s_bd = s_scr[i, p]
            stacked = jnp.where(bd_r, s_bd, pltpu.roll(s_bd, HEAD_DIM, 1))
            s_out_ref[i, p] = stacked[:, :HEAD_DIM]


def _retention(p3, cos_t, sin_t, tables, gn_w, s0, *, chunk, key_rows, nb):
    b, t, _ = p3.shape
    nc = t // chunk
    nb = min(nb, b)
    dmat, q_dec, k_dec, c_dec = tables
    pairs = RET_HEADS // 2
    s0 = s0.reshape(b, pairs, PAIR, HEAD_DIM)
    s0_wide = jnp.concatenate([s0, s0], axis=-1)
    col = lambda j: pl.BlockSpec((nb, chunk, RET_WIDTH), lambda i, c, j=j: (i, c, j))
    fixed2 = lambda a: pl.BlockSpec(a.shape, lambda i, c: (0, 0))
    state_spec = pl.BlockSpec((nb, pairs, PAIR, PAIR), lambda i, c: (i, 0, 0, 0))
    out_state_spec = pl.BlockSpec((nb, pairs, PAIR, HEAD_DIM), lambda i, c: (i, 0, 0, 0))
    o, s_new = pl.pallas_call(
        functools.partial(_retention_kernel, key_rows=key_rows),
        grid=(b // nb, nc),
        in_specs=[
            col(0), col(1), col(2), col(3),
            pl.BlockSpec((chunk, PAIR), lambda i, c: (c, 0)),
            pl.BlockSpec((chunk, PAIR), lambda i, c: (c, 0)),
            pl.BlockSpec(dmat.shape, lambda i, c: (0, 0, 0)),
            fixed2(q_dec), fixed2(k_dec), fixed2(c_dec), fixed2(gn_w),
            state_spec,
        ],
        out_specs=[
            pl.BlockSpec((nb, chunk, RET_WIDTH), lambda i, c: (i, c, 0)),
            out_state_spec,
        ],
        out_shape=[
            jax.ShapeDtypeStruct((b, t, RET_WIDTH), BF16),
            jax.ShapeDtypeStruct((b, pairs, PAIR, HEAD_DIM), F32),
        ],
        scratch_shapes=[pltpu.VMEM((nb, pairs, PAIR, PAIR), F32)],
        compiler_params=pltpu.CompilerParams(
            dimension_semantics=("arbitrary", "arbitrary"), vmem_limit_bytes=VMEM_LIMIT),
        name="retention",
    )(p3, p3, p3, p3, cos_t, sin_t, dmat, q_dec, k_dec, c_dec, gn_w, s0_wide)
    return o, s_new.reshape(b, RET_HEADS, HEAD_DIM, HEAD_DIM)


def _softplus_parts(z):
    u = jnp.exp2(jnp.minimum(z * LOG2_E, 126.0))
    sp = jnp.maximum(jnp.log(1.0 + u), z)
    return sp, z - sp


def _later_sum(sp, upper):
    hi, lo = _split_hi_lo(sp)
    return _dot(jnp.concatenate([hi, lo], axis=1), jnp.concatenate([upper, upper], axis=0))


def _strict_upper(n):
    r = lax.broadcasted_iota(jnp.int32, (n, n), 0)
    c = lax.broadcasted_iota(jnp.int32, (n, n), 1)
    return jnp.where(r > c, 1.0, 0.0).astype(BF16)


def _sb_prompt_kernel(bias_ref, q_ref, k_ref, v_ref, g_ref, o_ref, kb_scr, vb_scr, acc_scr, car_scr,
                      z00_scr, z01_scr, z10_scr, z11_scr, w0_scr, w1_scr):
    hp = pl.program_id(1)
    qi = pl.program_id(2)
    tq, tk = SB_TQ, SB_TK

    @pl.when(qi == 0)
    def _():
        for c in range(kb_scr.shape[0]):
            kb_scr[c] = k_ref[:, c * tk:(c + 1) * tk].astype(BF16)
            vb_scr[c] = v_ref[:, c * tk:(c + 1) * tk].astype(BF16)

    q = q_ref[...] * (HEAD_DIM ** -0.5)
    lane = lax.broadcasted_iota(jnp.int32, (tq, PAIR), 1)
    first = lane < HEAD_DIM
    zero = jnp.zeros((tq, PAIR), F32)
    q2 = jnp.concatenate([jnp.where(first, q, zero), jnp.where(first, zero, q)], axis=0).astype(BF16)
    row2 = lax.broadcasted_iota(jnp.int32, (2 * tq, 1), 0)
    bias2 = jnp.where(row2 < tq, bias_ref[2 * hp], bias_ref[2 * hp + 1])
    upper = _strict_upper(tk)
    col = lax.broadcasted_iota(jnp.int32, (2 * tq, tk), 1)
    row = lax.broadcasted_iota(jnp.int32, (2 * tq, tk), 0) % tq
    acc_scr[...] = jnp.zeros_like(acc_scr)
    car_scr[...] = jnp.zeros_like(car_scr)

    def scores(tiles):
        return [_dot(q2, kb_scr[j]) for j in tiles]

    def weights(zs, masks):
        carry = car_scr[...]
        sps, lbs, carries = [], [], []
        for z, mask in zip(zs, masks):
            sp, lb = _softplus_parts(z + bias2)
            if mask is not None:
                sp = jnp.where(mask, sp, 0.0)
            carries.append(carry)
            carry = carry + jnp.sum(sp, axis=-1, keepdims=True)
            sps.append(sp)
            lbs.append(lb)
        car_scr[...] = carry
        later = [_dot(sp.astype(BF16), upper) for sp in sps]
        out = []
        for lb, lt, cr, mask in zip(lbs, later, carries, masks):
            a = jnp.exp(lb - lt - cr)
            if mask is not None:
                a = jnp.where(mask, a, 0.0)
            out.append(a.astype(BF16))
        return out

    def accumulate(ws, tiles):
        acc = acc_scr[...]
        for w, j in zip(ws, tiles):
            acc = acc + _dot_nt(w, vb_scr[j])
        acc_scr[...] = acc

    n_pairs = qi
    top = 2 * qi - 1
    z_sets = ((z00_scr, z01_scr), (z10_scr, z11_scr))
    w_sets = (w0_scr, w1_scr)
    for z_ref, z in zip(z_sets[0], scores([jnp.maximum(top, 0), jnp.maximum(top - 1, 0)])):
        z_ref[...] = z

    diag = [2 * qi + 1, 2 * qi]
    masks = [(col + tk) < row, col < row]
    for t, w in enumerate(weights(scores(diag), masks)):
        w_sets[1][t] = w

    def step(i, cur, oth):
        j = top - 2 * i
        prev = [j + 2, j + 1]
        nxt = [jnp.maximum(j - 2, 0), jnp.maximum(j - 3, 0)]
        for z_ref, z in zip(z_sets[oth], scores(nxt)):
            z_ref[...] = z
        accumulate([w_sets[oth][0], w_sets[oth][1]], prev)
        row_carry = car_scr[...]
        args = []
        for z_ref in z_sets[cur]:
            sp, lb = _softplus_parts(z_ref[...] + bias2)
            later = _dot(sp.astype(BF16), upper)
            args.append((lb, later, row_carry))
            row_carry = row_carry + jnp.sum(sp, axis=-1, keepdims=True)
        car_scr[...] = row_carry
        for t, (lb, later, cr) in enumerate(args):
            w_sets[cur][t] = jnp.exp(lb - later - cr).astype(BF16)

    def body(k, carry):
        step(2 * k, 0, 1)
        step(2 * k + 1, 1, 0)
        return carry

    lax.fori_loop(0, n_pairs // 2, body, 0)

    @pl.when(n_pairs % 2 == 1)
    def _():
        step(n_pairs - 1, 0, 1)
        accumulate([w_sets[0][0], w_sets[0][1]], [1, 0])

    @pl.when(n_pairs % 2 == 0)
    def _():
        accumulate([w_sets[1][0], w_sets[1][1]], [1, 0])

    acc = acc_scr[...]
    o = jnp.where(first, acc[:tq], acc[tq:])
    o_ref[...] = (o * _silu(g_ref[...])).astype(o_ref.dtype)


def _sb_prompt(pa3, kt_buf, vt_buf, pb3, bias, layer):
    b, t, _ = pa3.shape
    assert SB_TQ == 2 * SB_TK and t % SB_TQ == 0
    nq = t // SB_TQ
    pairs = SB_HEADS // 2
    qcol = lambda off: pl.BlockSpec(
        (None, SB_TQ, PAIR), lambda i, hp, qi, off=off: (i, qi, off // PAIR + hp))
    kt_spec = pl.BlockSpec((None, None, PAIR, t), lambda i, hp, qi: (layer, i, hp, 0))
    return pl.pallas_call(
        _sb_prompt_kernel,
        grid=(b, pairs, nq),
        in_specs=[
            pl.BlockSpec(memory_space=pltpu.SMEM),
            qcol(A_SQ), kt_spec, kt_spec, qcol(B_SG),
        ],
        out_specs=pl.BlockSpec((None, SB_TQ, PAIR), lambda i, hp, qi: (i, qi, hp)),
        out_shape=jax.ShapeDtypeStruct((b, t, SB_WIDTH), BF16),
        scratch_shapes=[
            pltpu.VMEM((t // SB_TK, PAIR, SB_TK), BF16), pltpu.VMEM((t // SB_TK, PAIR, SB_TK), BF16),
            pltpu.VMEM((2 * SB_TQ, PAIR), F32), pltpu.VMEM((2 * SB_TQ, 1), F32),
        ] + [pltpu.VMEM((2 * SB_TQ, SB_TK), F32)] * 4 + [pltpu.VMEM((2, 2 * SB_TQ, SB_TK), BF16)] * 2,
        compiler_params=pltpu.CompilerParams(
            dimension_semantics=("arbitrary", "arbitrary", "arbitrary"),
            vmem_limit_bytes=VMEM_LIMIT),
        name="sb_prompt",
    )(bias, pa3, kt_buf, vt_buf, pb3)


def _sb_sample_kernel(pt_ref, bias_ref, q_ref, kn_ref, vn_ref, g_ref, *rest, n_pages, page):
    del pt_ref
    nb = q_ref.shape[0]
    k_refs = rest[:nb * n_pages]
    v_refs = rest[nb * n_pages:2 * nb * n_pages]
    o_ref = rest[2 * nb * n_pages]
    rows = SAMPLE_PAD
    nrow = SB_HEADS * rows
    lane_head = lax.broadcasted_iota(jnp.int32, (rows, SB_WIDTH), 1) // HEAD_DIM
    zero = jnp.zeros((rows, SB_WIDTH), F32)
    row = lax.broadcasted_iota(jnp.int32, (nrow, page), 0)
    col = lax.broadcasted_iota(jnp.int32, (nrow, page), 1)
    bias = jnp.zeros((nrow, page), F32)
    for h in range(SB_HEADS):
        bias = jnp.where(row // rows == h, bias_ref[h], bias)
    causal = col < (row % rows)
    upper = _strict_upper(page)
    pad = jnp.zeros((page - rows, SB_WIDTH), BF16)

    for s in range(nb):
        q = q_ref[s] * (HEAD_DIM ** -0.5)
        qbd = jnp.concatenate(
            [jnp.where(lane_head == h, q, zero) for h in range(SB_HEADS)], axis=0).astype(BF16)
        new_k = jnp.concatenate([kn_ref[s].astype(BF16), pad], axis=0)
        new_v = jnp.concatenate([vn_ref[s].astype(BF16), pad], axis=0)
        zs = [_dot_nt(qbd, new_k)]
        for j in reversed(range(n_pages)):
            zs.append(_dot(qbd, k_refs[s * n_pages + j][...].astype(BF16)))

        sps, lbs = [], []
        for i, z in enumerate(zs):
            sp, lb = _softplus_parts(z + bias)
            if i == 0:
                sp = jnp.where(causal, sp, 0.0)
            sps.append(sp)
            lbs.append(lb)
        halves = []
        for sp in sps:
            halves.extend(_split_hi_lo(sp))
        later = _dot(jnp.concatenate(halves, axis=0), upper)
        carry = jnp.zeros((nrow, 1), F32)
        acc = jnp.zeros((nrow, SB_WIDTH), F32)
        for i, (sp, lb) in enumerate(zip(sps, lbs)):
            within = (later[2 * i * nrow:(2 * i + 1) * nrow]
                      + later[(2 * i + 1) * nrow:(2 * i + 2) * nrow])
            a = jnp.exp(lb - within - carry)
            if i == 0:
                acc = acc + _dot(jnp.where(causal, a, 0.0).astype(BF16), new_v)
            else:
                v_page = v_refs[s * n_pages + n_pages - i][...].astype(BF16)
                acc = acc + _dot_nt(a.astype(BF16), v_page)
            carry = carry + jnp.sum(sp, axis=-1, keepdims=True)
        out = jnp.zeros((rows, SB_WIDTH), F32)
        for h in range(SB_HEADS):
            out = out + jnp.where(lane_head == h, acc[h * rows:(h + 1) * rows], 0.0)
        o_ref[s] = (out * _silu(g_ref[s])).astype(o_ref.dtype)


def _sb_sample(pa3, k3, v3, pb3, cache_k, cache_v, page_table, bias, layer):
    b = pa3.shape[0]
    nb = min(SB_NB_SAMPLE, b)
    n_pages = page_table.shape[1]
    page = cache_k.shape[3]
    tok = lambda off: pl.BlockSpec(
        (nb, SAMPLE_PAD, SB_WIDTH), lambda i, pt, off=off: (i, 0, off // SB_WIDTH))
    page_spec = lambda s, j: pl.BlockSpec(
        (None, None, SB_WIDTH, page), lambda i, pt, s=s, j=j: (layer, pt[nb * i + s, j], 0, 0))
    page_specs = [page_spec(s, j) for s in range(nb) for j in range(n_pages)]
    grid_spec = pltpu.PrefetchScalarGridSpec(
        num_scalar_prefetch=1,
        grid=(b // nb,),
        in_specs=[pl.BlockSpec(memory_space=pltpu.SMEM),
                  tok(A_SQ), tok(0), tok(0), tok(B_SG)] + page_specs * 2,
        out_specs=pl.BlockSpec((nb, SAMPLE_PAD, SB_WIDTH), lambda i, pt: (i, 0, 0)),
    )
    n_in = nb * n_pages
    return pl.pallas_call(
        functools.partial(_sb_sample_kernel, n_pages=n_pages, page=page),
        grid_spec=grid_spec,
        out_shape=jax.ShapeDtypeStruct((b, SAMPLE_PAD, SB_WIDTH), BF16),
        compiler_params=pltpu.CompilerParams(
            dimension_semantics=("arbitrary",), vmem_limit_bytes=VMEM_LIMIT),
        name="sb_sample",
    )(page_table, bias, pa3, k3, v3, pb3, *([cache_k] * n_in), *([cache_v] * n_in))


def _conv_tail(y, lnw_ref, lnb_ref, pw_ref, gate):
    mu = jnp.mean(y, axis=-1, keepdims=True)
    d = y - mu
    var = jnp.mean(d * d, axis=-1, keepdims=True)
    z = d * lax.rsqrt(var + EPS) * lnw_ref[...] + lnb_ref[...]
    return _dot(_silu(z).astype(BF16), pw_ref[...]) * _silu(gate)


def _conv_prompt_kernel(ca_ref, cb_ref, cg_ref, dww_ref, dwb_ref, lnw_ref, lnb_ref, pw_ref,
                        o_ref, st_ref, ext_scr, phase_scr):
    ti = pl.program_id(1)
    tt = ca_ref.shape[0]

    @pl.when(ti == 0)
    def _():
        ext_scr[0:CONV_HALO, :] = jnp.zeros((CONV_HALO, CONV_WIDTH), F32)

    @pl.when(ti > 0)
    def _():
        ext_scr[0:CONV_HALO, :] = ext_scr[tt:tt + CONV_HALO, :]

    u = ca_ref[...] * _sigmoid(cb_ref[...])
    ext_scr[CONV_HALO:CONV_HALO + tt, :] = u
    first_tap = CONV_HALO - (CONV_K - 1)
    for r in range(1, SUBLANES):
        phase_scr[r - 1] = ext_scr[r:r + phase_scr.shape[1], :]
    y = jnp.zeros((tt, CONV_WIDTH), F32) + dwb_ref[...]
    for j in range(CONV_K):
        r = (first_tap + j) % SUBLANES
        a = first_tap + j - r
        window = ext_scr[a:a + tt, :] if r == 0 else phase_scr[r - 1, a:a + tt, :]
        y = y + window * dww_ref[j:j + 1, :]
    o_ref[...] = _conv_tail(y, lnw_ref, lnb_ref, pw_ref, cg_ref[...]).astype(o_ref.dtype)

    @pl.when(ti == pl.num_programs(1) - 1)
    def _():
        st_ref[...] = ext_scr[CONV_HALO + tt - (CONV_K - 1):CONV_HALO + tt, :]


def _conv_prompt(p3, dw_w, dw_b, ln_w, ln_b, pw_bf16):
    b, t, _ = p3.shape
    tt = min(CONV_TT, t)
    col = lambda off: pl.BlockSpec(
        (None, tt, CONV_WIDTH), lambda i, c, off=off: (i, c, off // CONV_WIDTH))
    fixed = lambda a: pl.BlockSpec(a.shape, lambda i, c: (0, 0))
    return pl.pallas_call(
        _conv_prompt_kernel,
        grid=(b, t // tt),
        in_specs=[col(B_CA), col(B_CB), col(B_CG),
                  fixed(dw_w), fixed(dw_b), fixed(ln_w), fixed(ln_b), fixed(pw_bf16)],
        out_specs=[
            pl.BlockSpec((None, tt, CONV_WIDTH), lambda i, c: (i, c, 0)),
            pl.BlockSpec((None, CONV_K - 1, CONV_WIDTH), lambda i, c: (i, 0, 0)),
        ],
        out_shape=[
            jax.ShapeDtypeStruct((b, t, CONV_WIDTH), BF16),
            jax.ShapeDtypeStruct((b, CONV_K - 1, CONV_WIDTH), F32),
        ],
        scratch_shapes=[
            pltpu.VMEM((tt + CONV_HALO, CONV_WIDTH), F32),
            pltpu.VMEM((SUBLANES - 1, tt + CONV_HALO - SUBLANES, CONV_WIDTH), F32),
        ],
        compiler_params=pltpu.CompilerParams(
            dimension_semantics=("arbitrary", "arbitrary"), vmem_limit_bytes=VMEM_LIMIT),
        name="conv_prompt",
    )(p3, p3, p3, dw_w, dw_b, ln_w, ln_b, pw_bf16)


def _conv_sample_kernel(ca_ref, cb_ref, cg_ref, buf_ref, dww_ref, dwb_ref, lnw_ref, lnb_ref,
                        pw_ref, o_ref, st_ref, *, n_tok):
    hist = CONV_K - 1
    rows = [buf_ref[:, j, :] for j in range(hist)]
    rows += [ca_ref[:, t, :] * _sigmoid(cb_ref[:, t, :]) for t in range(n_tok)]
    for t in range(n_tok):
        y = jnp.zeros_like(rows[0]) + dwb_ref[...]
        for j in range(CONV_K):
            y = y + rows[t + j] * dww_ref[j:j + 1, :]
        o_ref[:, t, :] = _conv_tail(y, lnw_ref, lnb_ref, pw_ref, cg_ref[:, t, :]).astype(o_ref.dtype)
    for t in range(n_tok, SAMPLE_PAD):
        o_ref[:, t, :] = jnp.zeros(rows[0].shape, o_ref.dtype)
    for j in range(hist):
        st_ref[:, j, :] = rows[n_tok + j]


def _conv_sample(p3, buf, dw_w, dw_b, ln_w, ln_b, pw_bf16, n_tok):
    b = p3.shape[0]
    bb = min(32, b)
    col = lambda off: pl.BlockSpec(
        (bb, SAMPLE_PAD, CONV_WIDTH), lambda i, off=off: (i, 0, off // CONV_WIDTH))
    fixed = lambda a: pl.BlockSpec(a.shape, lambda i: (0, 0))
    st_spec = pl.BlockSpec((bb, CONV_K - 1, CONV_WIDTH), lambda i: (i, 0, 0))
    return pl.pallas_call(
        functools.partial(_conv_sample_kernel, n_tok=n_tok),
        grid=(b // bb,),
        in_specs=[col(B_CA), col(B_CB), col(B_CG), st_spec,
                  fixed(dw_w), fixed(dw_b), fixed(ln_w), fixed(ln_b), fixed(pw_bf16)],
        out_specs=[pl.BlockSpec((bb, SAMPLE_PAD, CONV_WIDTH), lambda i: (i, 0, 0)), st_spec],
        out_shape=[
            jax.ShapeDtypeStruct((b, SAMPLE_PAD, CONV_WIDTH), BF16),
            jax.ShapeDtypeStruct(buf.shape, F32),
        ],
        compiler_params=pltpu.CompilerParams(
            dimension_semantics=("arbitrary",), vmem_limit_bytes=VMEM_LIMIT),
        name="conv_sample",
    )(p3, p3, p3, buf, dw_w, dw_b, ln_w, ln_b, pw_bf16)


def kernel(x_prompt, x_sample, state_ret, cache_k, cache_v, state_conv, page_table, norm_pre, w_in,
           ret_gn_w, sb_bias, conv_dw_w, conv_dw_b, conv_ln_w, conv_ln_b, conv_pw_w, w_out, norm_post):
    bp, tp, d = x_prompt.shape
    bs, ts, _ = x_sample.shape
    depth = w_in.shape[0]
    n_pages = page_table.shape[1]
    page = cache_k.shape[2]
    past_len = n_pages * page
    assert ts <= SAMPLE_PAD and tp % RET_CHUNK == 0

    xp = x_prompt.reshape(bp * tp, d)
    xs = jnp.pad(x_sample, ((0, 0), (0, SAMPLE_PAD - ts), (0, 0))).reshape(bs * SAMPLE_PAD, d)
    ck = cache_k.transpose(0, 1, 3, 4, 2).reshape(depth, cache_k.shape[1], SB_WIDTH, page)
    cv = cache_v.transpose(0, 1, 3, 4, 2).reshape(depth, cache_v.shape[1], SB_WIDTH, page)

    cos_p, sin_p = _rope_tables(jnp.arange(tp))
    cos_s, sin_s = _rope_tables(past_len + jnp.arange(SAMPLE_PAD))
    tab_p = _retention_tables(RET_CHUNK, RET_CHUNK, RET_CHUNK)
    tab_s = _retention_tables(ts, SAMPLE_PAD, PAIR)
    s0_p = jnp.zeros((bp, RET_HEADS, HEAD_DIM, HEAD_DIM), F32)

    kt_buf = jnp.zeros((depth, bp, SB_WIDTH, tp), F32)
    vt_buf = jnp.zeros((depth, bp, SB_WIDTH, tp), F32)
    outs = [[] for _ in range(6)]
    for l in range(depth):
        w_in_b = w_in[l].astype(BF16)
        w_out_b = w_out[l].astype(BF16)
        pw_b = conv_pw_w[l].astype(BF16)
        g_pre, g_post = norm_pre[l][None, :], norm_post[l][None, :]
        gn_w = ret_gn_w[l][None, :]
        dw_b, ln_w, ln_b = conv_dw_b[l][None, :], conv_ln_w[l][None, :], conv_ln_b[l][None, :]

        pa, pb, kt_buf, vt_buf = _inproj_prompt(xp, g_pre, w_in_b, kt_buf, vt_buf, l)
        pa, pb = pa.reshape(bp, tp, -1), pb.reshape(bp, tp, -1)
        or_p, s_p = _retention(pa, cos_p, sin_p, tab_p, gn_w, s0_p, chunk=RET_CHUNK,
                               key_rows=RET_CHUNK, nb=RET_NB_PROMPT)
        os_p = _sb_prompt(pa, kt_buf, vt_buf, pb, sb_bias[l], l)
        oc_p, cst_p = _conv_prompt(pb, conv_dw_w[l], dw_b, ln_w, ln_b, pw_b)
        xp = _outproj(or_p.reshape(bp * tp, -1), os_p.reshape(bp * tp, -1), oc_p.reshape(bp * tp, -1),
                      xp, w_out_b, g_post)

        sa, k_s, v_s, sb = [a.reshape(bs, SAMPLE_PAD, -1) for a in _inproj(xs, g_pre, w_in_b)]
        or_s, s_s = _retention(sa, cos_s, sin_s, tab_s, gn_w, state_ret[l],
                               chunk=SAMPLE_PAD, key_rows=PAIR, nb=RET_NB_SAMPLE)
        os_s = _sb_sample(sa, k_s, v_s, sb, ck, cv, page_table, sb_bias[l], l)
        oc_s, cst_s = _conv_sample(sb, state_conv[l], conv_dw_w[l], dw_b, ln_w, ln_b, pw_b, ts)
        n_s = bs * SAMPLE_PAD
        xs = _outproj(or_s.reshape(n_s, -1), os_s.reshape(n_s, -1), oc_s.reshape(n_s, -1),
                      xs, w_out_b, g_post)

        outs[0].append(s_p)
        outs[1].append(s_s)
        outs[2].append(k_s[:, :ts].reshape(bs, ts, SB_HEADS, HEAD_DIM))
        outs[3].append(v_s[:, :ts].reshape(bs, ts, SB_HEADS, HEAD_DIM))
        outs[4].append(cst_p)
        outs[5].append(cst_s)

    y_prompt = xp.reshape(bp, tp, d)
    y_sample = xs.reshape(bs, SAMPLE_PAD, d)[:, :ts]
    ret_p, ret_s, k_rows_s, v_rows_s, conv_p, conv_s = [jnp.stack(o) for o in outs]
    rows_p = [a.reshape(depth, bp, SB_HEADS, HEAD_DIM, tp).transpose(0, 1, 4, 2, 3)
              for a in (kt_buf, vt_buf)]
    return (y_prompt, y_sample, ret_p, ret_s, rows_p[0], rows_p[1], k_rows_s, v_rows_s, conv_p, conv_s)
```

```python
import functools

import numpy as np
import jax
import jax.numpy as jnp
from jax import lax
from jax.experimental import pallas as pl
from jax.experimental.pallas import tpu as pltpu

F32 = jnp.float32
BF16 = jnp.bfloat16

HEAD_DIM = 64
PAIR = 2 * HEAD_DIM
RET_HEADS = 6
SB_HEADS = 6
RET_WIDTH = RET_HEADS * HEAD_DIM
SB_WIDTH = SB_HEADS * HEAD_DIM
CONV_WIDTH = 256
CONV_K = 31
IN_WIDTH = 4 * RET_WIDTH + 4 * SB_WIDTH + 3 * CONV_WIDTH
RET_CHUNK = 128
ROPE_BASE = 10000.0
EPS = 1e-6
LOG2_E = 1.4426950408889634
SUBLANES = 8
SAMPLE_PAD = SUBLANES

INPROJ_WIDTHS = (1920, SB_WIDTH, SB_WIDTH, 1152)
A_SQ = 1536
B_CA, B_CB, B_CG, B_SG = 0, 256, 512, 768
INPROJ_CHUNKS = ((0, 768, 0, 0), (768, 768, 0, 768), (1536, 384, 0, 1536), (1920, 384, 1, 0),
                 (2304, 384, 2, 0), (3072, 768, 3, 0), (2688, 384, 3, 768))

ROW_TILE = 512
OUT_ROW_TILE = 1024
SB_TQ = 512
SB_TK = 256
RET_NB_PROMPT = 2
RET_NB_SAMPLE = 8
SB_NB_SAMPLE = 1
CONV_TT = 512
CONV_HALO = 32
VMEM_LIMIT = 56 * 1024 * 1024


def _dot(a, b):
    return jnp.dot(a, b, preferred_element_type=F32)


def _dot_nt(a, b):
    return lax.dot_general(a, b, (((1,), (1,)), ((), ())), preferred_element_type=F32)


def _dot_tn(a, b):
    return lax.dot_general(a, b, (((0,), (0,)), ((), ())), preferred_element_type=F32)


def _sigmoid(x):
    return 1.0 / (1.0 + jnp.exp(-x))


def _silu(x):
    return x * _sigmoid(x)


def _inproj_kernel(x_ref, g_ref, w_ref, *out_refs):
    x = x_ref[...]
    ms = jnp.mean(x * x, axis=-1, keepdims=True)
    h = (x * lax.rsqrt(ms + EPS) * g_ref[...]).astype(BF16)
    for src, width, dst, col in INPROJ_CHUNKS:
        out_refs[dst][:, col:col + width] = _dot(h, w_ref[:, src:src + width])


def _inproj(x2d, g, w_bf16):
    n, d = x2d.shape
    tm = min(ROW_TILE, n)
    widths = INPROJ_WIDTHS
    return pl.pallas_call(
        _inproj_kernel,
        grid=(n // tm,),
        in_specs=[
            pl.BlockSpec((tm, d), lambda i: (i, 0)),
            pl.BlockSpec((1, d), lambda i: (0, 0)),
            pl.BlockSpec((d, IN_WIDTH), lambda i: (0, 0)),
        ],
        out_specs=[pl.BlockSpec((tm, w), lambda i: (i, 0)) for w in widths],
        out_shape=[jax.ShapeDtypeStruct((n, w), F32) for w in widths],
        compiler_params=pltpu.CompilerParams(
            dimension_semantics=("arbitrary",), vmem_limit_bytes=VMEM_LIMIT),
        name="inproj",
    )(x2d, g, w_bf16)


def _inproj_prompt_kernel(x_ref, g_ref, w_ref, wkt_ref, wvt_ref, kt_in, vt_in,
                          a_ref, b_ref, kt_ref, vt_ref):
    del kt_in, vt_in
    x = x_ref[...]
    ms = jnp.mean(x * x, axis=-1, keepdims=True)
    h = (x * lax.rsqrt(ms + EPS) * g_ref[...]).astype(BF16)
    for src, width, dst, col in INPROJ_CHUNKS:
        if dst == 0:
            a_ref[:, col:col + width] = _dot(h, w_ref[:, src:src + width])
        elif dst == 3:
            b_ref[:, col:col + width] = _dot(h, w_ref[:, src:src + width])
    kt_ref[...] = _dot_nt(wkt_ref[...], h)
    vt_ref[...] = _dot_nt(wvt_ref[...], h)


def _inproj_prompt(x2d, g, w_bf16, kt_buf, vt_buf, layer):
    n, d = x2d.shape
    t = kt_buf.shape[3]
    tm = min(ROW_TILE, t)
    per_seq = t // tm
    off_k, off_v = INPROJ_CHUNKS[3][0], INPROJ_CHUNKS[4][0]
    wkt = w_bf16[:, off_k:off_k + SB_WIDTH].T
    wvt = w_bf16[:, off_v:off_v + SB_WIDTH].T
    row = lambda i: (i, 0)
    fixed = lambda i: (0, 0)
    t_spec = pl.BlockSpec((None, None, SB_WIDTH, tm), lambda i: (layer, i // per_seq, 0, i % per_seq))
    return pl.pallas_call(
        _inproj_prompt_kernel,
        grid=(n // tm,),
        in_specs=[
            pl.BlockSpec((tm, d), row),
            pl.BlockSpec((1, d), fixed),
            pl.BlockSpec((d, IN_WIDTH), fixed),
            pl.BlockSpec(wkt.shape, fixed),
            pl.BlockSpec(wvt.shape, fixed),
            pl.BlockSpec(memory_space=pl.ANY),
            pl.BlockSpec(memory_space=pl.ANY),
        ],
        out_specs=[pl.BlockSpec((tm, INPROJ_WIDTHS[0]), row), pl.BlockSpec((tm, INPROJ_WIDTHS[3]), row),
                   t_spec, t_spec],
        out_shape=[jax.ShapeDtypeStruct((n, INPROJ_WIDTHS[0]), F32),
                   jax.ShapeDtypeStruct((n, INPROJ_WIDTHS[3]), F32),
                   jax.ShapeDtypeStruct(kt_buf.shape, F32), jax.ShapeDtypeStruct(vt_buf.shape, F32)],
        input_output_aliases={5: 2, 6: 3},
        compiler_params=pltpu.CompilerParams(
            dimension_semantics=("arbitrary",), vmem_limit_bytes=VMEM_LIMIT),
        name="inproj_prompt",
    )(x2d, g, w_bf16, wkt, wvt, kt_buf, vt_buf)


def _outproj_kernel(or_ref, os_ref, oc_ref, x_ref, w_ref, g_ref, o_ref):
    mix = jnp.concatenate([or_ref[...], os_ref[...], oc_ref[...]], axis=-1)
    y = _dot(mix, w_ref[...])
    ms = jnp.mean(y * y, axis=-1, keepdims=True)
    o_ref[...] = x_ref[...] + y * lax.rsqrt(ms + EPS) * g_ref[...]


def _outproj(o_r, o_s, o_c, x2d, w_bf16, g):
    n, d = x2d.shape
    tm = min(OUT_ROW_TILE, n)
    row = lambda i: (i, 0)
    fixed = lambda i: (0, 0)
    return pl.pallas_call(
        _outproj_kernel,
        grid=(n // tm,),
        in_specs=[
            pl.BlockSpec((tm, RET_WIDTH), row),
            pl.BlockSpec((tm, SB_WIDTH), row),
            pl.BlockSpec((tm, CONV_WIDTH), row),
            pl.BlockSpec((tm, d), row),
            pl.BlockSpec(w_bf16.shape, fixed),
            pl.BlockSpec((1, d), fixed),
        ],
        out_specs=pl.BlockSpec((tm, d), row),
        out_shape=jax.ShapeDtypeStruct((n, d), F32),
        compiler_params=pltpu.CompilerParams(
            dimension_semantics=("arbitrary",), vmem_limit_bytes=VMEM_LIMIT),
        name="outproj",
    )(o_r, o_s, o_c, x2d, w_bf16, g)


def _rope_tables(pos):
    half = HEAD_DIM // 2
    inv = jnp.power(ROPE_BASE, -2.0 * jnp.arange(half, dtype=F32) / HEAD_DIM)
    ang = pos.astype(F32)[:, None] * inv[None, :]
    cos, sin = jnp.cos(ang), jnp.sin(ang)
    cos_t = jnp.concatenate([cos, cos, cos, cos], axis=-1)
    sin_t = jnp.concatenate([-sin, sin, -sin, sin], axis=-1)
    return cos_t, sin_t


def _retention_tables(c_valid, rows, keys):
    h = np.arange(RET_HEADS, dtype=np.float64)
    lg = np.log(1.0 - np.power(2.0, -5.0 - h))
    c = np.arange(rows, dtype=np.float64)
    s = np.arange(keys, dtype=np.float64)
    diff = c[:, None] - s[None, :]
    ok = (diff >= 0) & (s[None, :] < c_valid) & (c[:, None] < c_valid)
    dmat = np.where(ok[None], np.exp(np.maximum(diff, 0.0)[None] * lg[:, None, None]), 0.0)
    q_dec = np.exp((c[:, None] + 1.0) * lg[None, :])
    k_dec = np.where(c[:, None] < c_valid, np.exp((c_valid - 1.0 - c)[:, None] * lg[None, :]), 0.0)
    c_dec = np.exp(c_valid * lg)[None, :]
    lanes = lambda t: np.repeat(t, HEAD_DIM, axis=-1)
    return (jnp.asarray(dmat, F32), jnp.asarray(lanes(q_dec), F32),
            jnp.asarray(lanes(k_dec), F32), jnp.asarray(lanes(c_dec), F32))


def _retention_kernel(rq_ref, rk_ref, rv_ref, rg_ref, cos_ref, sin_ref, d_ref, qdec_ref,
                      kdec_ref, cdec_ref, gn_ref, s0_ref, o_ref, s_out_ref, s_scr, *, key_rows):
    c_idx = pl.program_id(1)
    nb, rows = rq_ref.shape[0], rq_ref.shape[1]
    lane = lax.broadcasted_iota(jnp.int32, (rows, PAIR), 1)
    first = lane < HEAD_DIM
    low_half = (lane % HEAD_DIM) < (HEAD_DIM // 2)
    bd_r = lax.broadcasted_iota(jnp.int32, (PAIR, PAIR), 0) < HEAD_DIM
    bd_c = lax.broadcasted_iota(jnp.int32, (PAIR, PAIR), 1) < HEAD_DIM
    block_diag = bd_r == bd_c

    @pl.when(c_idx == 0)
    def _():
        for i, p in [(i, p) for i in range(nb) for p in range(RET_HEADS // 2)]:
            s_scr[i, p] = jnp.where(block_diag, s0_ref[i, p], 0.0)
    cos_t = cos_ref[...]
    sin_t = sin_ref[...]

    def rope(x):
        swapped = jnp.where(low_half, pltpu.roll(x, PAIR - HEAD_DIM // 2, 1),
                            pltpu.roll(x, HEAD_DIM // 2, 1))
        return x * cos_t + swapped * sin_t

    def pad_keys(x):
        if key_rows == rows:
            return x
        return jnp.concatenate([x, jnp.zeros((key_rows - rows, PAIR), x.dtype)], axis=0)

    for i, p in [(i, p) for i in range(nb) for p in range(RET_HEADS // 2)]:
        sl = slice(PAIR * p, PAIR * (p + 1))
        q = rope(rq_ref[i, :, sl])
        k = rope(rk_ref[i, :, sl]) * (HEAD_DIM ** -0.5)
        qb = q.astype(BF16)
        kb = pad_keys(k.astype(BF16))
        vb = pad_keys(rv_ref[i, :, sl].astype(BF16))
        zero = jnp.zeros_like(qb)
        att0 = _dot_nt(jnp.where(first, qb, zero), kb) * d_ref[2 * p]
        att1 = _dot_nt(jnp.where(first, zero, qb), kb) * d_ref[2 * p + 1]
        s_old = s_scr[i, p]
        o = jnp.where(first, _dot(att0.astype(BF16), vb), _dot(att1.astype(BF16), vb))
        o = o + _dot((q * qdec_ref[:, sl]).astype(BF16), s_old.astype(BF16))
        kd = pad_keys((k * kdec_ref[:, sl]).astype(BF16))
        kv = _dot_tn(kd, vb)
        s_scr[i, p] = s_old * cdec_ref[:, sl] + jnp.where(block_diag, kv, 0.0)

        inv_d = 1.0 / HEAD_DIM
        mu0 = jnp.sum(jnp.where(first, o, 0.0), axis=-1, keepdims=True) * inv_d
        mu1 = jnp.sum(jnp.where(first, 0.0, o), axis=-1, keepdims=True) * inv_d
        dlt = o - jnp.where(first, mu0, mu1)
        sq = dlt * dlt
        v0 = jnp.sum(jnp.where(first, sq, 0.0), axis=-1, keepdims=True) * inv_d
        v1 = jnp.sum(jnp.where(first, 0.0, sq), axis=-1, keepdims=True) * inv_d
        y = dlt * lax.rsqrt(jnp.where(first, v0, v1) + EPS) * gn_ref[:, sl]
        o_ref[i, :, sl] = (y * _silu(rg_ref[i, :, sl])).astype(o_ref.dtype)

    @pl.when(c_idx == pl.num_programs(1) - 1)
    def _():
        for i, p in [(i, p) for i in range(nb) for p in range(RET_HEADS // 2)]:
            s_bd = s_scr[i, p]
            stacked = jnp.where(bd_r, s_bd, pltpu.roll(s_bd, HEAD_DIM, 1))
            s_out_ref[i, p] = stacked[:, :HEAD_DIM]


def _retention(p3, cos_t, sin_t, tables, gn_w, s0, *, chunk, key_rows, nb):
    b, t, _ = p3.shape
    nc = t // chunk
    nb = min(nb, b)
    dmat, q_dec, k_dec, c_dec = tables
    pairs = RET_HEADS // 2
    s0 = s0.reshape(b, pairs, PAIR, HEAD_DIM)
    s0_wide = jnp.concatenate([s0, s0], axis=-1)
    col = lambda j: pl.BlockSpec((nb, chunk, RET_WIDTH), lambda i, c, j=j: (i, c, j))
    fixed2 = lambda a: pl.BlockSpec(a.shape, lambda i, c: (0, 0))
    state_spec = pl.BlockSpec((nb, pairs, PAIR, PAIR), lambda i, c: (i, 0, 0, 0))
    out_state_spec = pl.BlockSpec((nb, pairs, PAIR, HEAD_DIM), lambda i, c: (i, 0, 0, 0))
    o, s_new = pl.pallas_call(
        functools.partial(_retention_kernel, key_rows=key_rows),
        grid=(b // nb, nc),
        in_specs=[
            col(0), col(1), col(2), col(3),
            pl.BlockSpec((chunk, PAIR), lambda i, c: (c, 0)),
            pl.BlockSpec((chunk, PAIR), lambda i, c: (c, 0)),
            pl.BlockSpec(dmat.shape, lambda i, c: (0, 0, 0)),
            fixed2(q_dec), fixed2(k_dec), fixed2(c_dec), fixed2(gn_w),
            state_spec,
        ],
        out_specs=[
            pl.BlockSpec((nb, chunk, RET_WIDTH), lambda i, c: (i, c, 0)),
            out_state_spec,
        ],
        out_shape=[
            jax.ShapeDtypeStruct((b, t, RET_WIDTH), BF16),
            jax.ShapeDtypeStruct((b, pairs, PAIR, HEAD_DIM), F32),
        ],
        scratch_shapes=[pltpu.VMEM((nb, pairs, PAIR, PAIR), F32)],
        compiler_params=pltpu.CompilerParams(
            dimension_semantics=("arbitrary", "arbitrary"), vmem_limit_bytes=VMEM_LIMIT),
        name="retention",
    )(p3, p3, p3, p3, cos_t, sin_t, dmat, q_dec, k_dec, c_dec, gn_w, s0_wide)
    return o, s_new.reshape(b, RET_HEADS, HEAD_DIM, HEAD_DIM)


def _softplus_parts(z):
    u = jnp.exp2(jnp.minimum(z * LOG2_E, 126.0))
    sp = jnp.maximum(jnp.log(1.0 + u), z)
    return sp, z - sp


def _strict_upper(n):
    r = lax.broadcasted_iota(jnp.int32, (n, n), 0)
    c = lax.broadcasted_iota(jnp.int32, (n, n), 1)
    return jnp.where(r > c, 1.0, 0.0).astype(BF16)


def _sb_prompt_kernel(bias_ref, q_ref, k_ref, v_ref, g_ref, o_ref, kb_scr, vb_scr, acc_scr, car_scr,
                      z00_scr, z01_scr, z10_scr, z11_scr, w0_scr, w1_scr):
    hp = pl.program_id(1)
    qi = pl.program_id(2)
    tq, tk = SB_TQ, SB_TK

    @pl.when(qi == 0)
    def _():
        for c in range(kb_scr.shape[0]):
            kb_scr[c] = k_ref[:, c * tk:(c + 1) * tk].astype(BF16)
            vb_scr[c] = v_ref[:, c * tk:(c + 1) * tk].astype(BF16)

    q = q_ref[...] * (HEAD_DIM ** -0.5)
    lane = lax.broadcasted_iota(jnp.int32, (tq, PAIR), 1)
    first = lane < HEAD_DIM
    zero = jnp.zeros((tq, PAIR), F32)
    q2 = jnp.concatenate([jnp.where(first, q, zero), jnp.where(first, zero, q)], axis=0).astype(BF16)
    row2 = lax.broadcasted_iota(jnp.int32, (2 * tq, 1), 0)
    bias2 = jnp.where(row2 < tq, bias_ref[2 * hp], bias_ref[2 * hp + 1])
    upper = _strict_upper(tk)
    col = lax.broadcasted_iota(jnp.int32, (2 * tq, tk), 1)
    row = lax.broadcasted_iota(jnp.int32, (2 * tq, tk), 0) % tq
    acc_scr[...] = jnp.zeros_like(acc_scr)
    car_scr[...] = jnp.zeros_like(car_scr)

    def scores(tiles):
        return [_dot(q2, kb_scr[j]) for j in tiles]

    def weights(zs, masks):
        carry = car_scr[...]
        sps, lbs, carries = [], [], []
        for z, mask in zip(zs, masks):
            sp, lb = _softplus_parts(z + bias2)
            if mask is not None:
                sp = jnp.where(mask, sp, 0.0)
            carries.append(carry)
            carry = carry + jnp.sum(sp, axis=-1, keepdims=True)
            sps.append(sp)
            lbs.append(lb)
        car_scr[...] = carry
        later = [_dot(sp.astype(BF16), upper) for sp in sps]
        out = []
        for lb, lt, cr, mask in zip(lbs, later, carries, masks):
            a = jnp.exp(lb - lt - cr)
            if mask is not None:
                a = jnp.where(mask, a, 0.0)
            out.append(a.astype(BF16))
        return out

    def accumulate(ws, tiles):
        acc = acc_scr[...]
        for w, j in zip(ws, tiles):
            acc = acc + _dot_nt(w, vb_scr[j])
        acc_scr[...] = acc

    n_pairs = qi
    top = 2 * qi - 1
    z_sets = ((z00_scr, z01_scr), (z10_scr, z11_scr))
    w_sets = (w0_scr, w1_scr)
    for z_ref, z in zip(z_sets[0], scores([jnp.maximum(top, 0), jnp.maximum(top - 1, 0)])):
        z_ref[...] = z

    diag = [2 * qi + 1, 2 * qi]
    masks = [(col + tk) < row, col < row]
    for t, w in enumerate(weights(scores(diag), masks)):
        w_sets[1][t] = w

    def step(i, cur, oth):
        j = top - 2 * i
        prev = [j + 2, j + 1]
        nxt = [jnp.maximum(j - 2, 0), jnp.maximum(j - 3, 0)]
        for z_ref, z in zip(z_sets[oth], scores(nxt)):
            z_ref[...] = z
        accumulate([w_sets[oth][0], w_sets[oth][1]], prev)
        row_carry = car_scr[...]
        args = []
        for z_ref in z_sets[cur]:
            sp, lb = _softplus_parts(z_ref[...] + bias2)
            later = _dot(sp.astype(BF16), upper)
            args.append((lb, later, row_carry))
            row_carry = row_carry + jnp.sum(sp, axis=-1, keepdims=True)
        car_scr[...] = row_carry
        for t, (lb, later, cr) in enumerate(args):
            w_sets[cur][t] = jnp.exp(lb - later - cr).astype(BF16)

    def body(k, carry):
        step(2 * k, 0, 1)
        step(2 * k + 1, 1, 0)
        return carry

    lax.fori_loop(0, n_pairs // 2, body, 0)

    @pl.when(n_pairs % 2 == 1)
    def _():
        step(n_pairs - 1, 0, 1)
        accumulate([w_sets[0][0], w_sets[0][1]], [1, 0])

    @pl.when(n_pairs % 2 == 0)
    def _():
        accumulate([w_sets[1][0], w_sets[1][1]], [1, 0])

    acc = acc_scr[...]
    o = jnp.where(first, acc[:tq], acc[tq:])
    o_ref[...] = (o * _silu(g_ref[...])).astype(o_ref.dtype)


def _sb_prompt(pa3, kt_buf, vt_buf, pb3, bias, layer):
    b, t, _ = pa3.shape
    assert SB_TQ == 2 * SB_TK and t % SB_TQ == 0
    nq = t // SB_TQ
    pairs = SB_HEADS // 2
    qcol = lambda off: pl.BlockSpec(
        (None, SB_TQ, PAIR), lambda i, hp, qi, off=off: (i, qi, off // PAIR + hp))
    kt_spec = pl.BlockSpec((None, None, PAIR, t), lambda i, hp, qi: (layer, i, hp, 0))
    return pl.pallas_call(
        _sb_prompt_kernel,
        grid=(b, pairs, nq),
        in_specs=[
            pl.BlockSpec(memory_space=pltpu.SMEM),
            qcol(A_SQ), kt_spec, kt_spec, qcol(B_SG),
        ],
        out_specs=pl.BlockSpec((None, SB_TQ, PAIR), lambda i, hp, qi: (i, qi, hp)),
        out_shape=jax.ShapeDtypeStruct((b, t, SB_WIDTH), BF16),
        scratch_shapes=[
            pltpu.VMEM((t // SB_TK, PAIR, SB_TK), BF16), pltpu.VMEM((t // SB_TK, PAIR, SB_TK), BF16),
            pltpu.VMEM((2 * SB_TQ, PAIR), F32), pltpu.VMEM((2 * SB_TQ, 1), F32),
        ] + [pltpu.VMEM((2 * SB_TQ, SB_TK), F32)] * 4 + [pltpu.VMEM((2, 2 * SB_TQ, SB_TK), BF16)] * 2,
        compiler_params=pltpu.CompilerParams(
            dimension_semantics=("arbitrary", "arbitrary", "arbitrary"),
            vmem_limit_bytes=VMEM_LIMIT),
        name="sb_prompt",
    )(bias, pa3, kt_buf, vt_buf, pb3)


def _sb_sample_kernel(pt_ref, bias_ref, q_ref, kn_ref, vn_ref, g_ref, *rest, n_pages, page):
    del pt_ref
    nb = q_ref.shape[0]
    k_refs = rest[:nb * n_pages]
    v_refs = rest[nb * n_pages:2 * nb * n_pages]
    o_ref = rest[2 * nb * n_pages]
    rows = SAMPLE_PAD
    nrow = SB_HEADS * rows
    lane_head = lax.broadcasted_iota(jnp.int32, (rows, SB_WIDTH), 1) // HEAD_DIM
    zero = jnp.zeros((rows, SB_WIDTH), F32)
    row = lax.broadcasted_iota(jnp.int32, (nrow, page), 0)
    col = lax.broadcasted_iota(jnp.int32, (nrow, page), 1)
    bias = jnp.zeros((nrow, page), F32)
    for h in range(SB_HEADS):
        bias = jnp.where(row // rows == h, bias_ref[h], bias)
    causal = col < (row % rows)
    upper = _strict_upper(page)
    pad = jnp.zeros((page - rows, SB_WIDTH), BF16)

    for s in range(nb):
        q = q_ref[s] * (HEAD_DIM ** -0.5)
        qbd = jnp.concatenate(
            [jnp.where(lane_head == h, q, zero) for h in range(SB_HEADS)], axis=0).astype(BF16)
        new_k = jnp.concatenate([kn_ref[s].astype(BF16), pad], axis=0)
        new_v = jnp.concatenate([vn_ref[s].astype(BF16), pad], axis=0)
        k_all = jnp.concatenate(
            [k_refs[s * n_pages + j][...].astype(BF16) for j in range(n_pages)], axis=1)
        v_all = jnp.concatenate(
            [v_refs[s * n_pages + j][...].T.astype(BF16) for j in range(n_pages)], axis=0)
        z_pages = _dot(qbd, k_all)
        zs = [_dot_nt(qbd, new_k)]
        for j in reversed(range(n_pages)):
            zs.append(z_pages[:, j * page:(j + 1) * page])

        sps, lbs = [], []
        for i, z in enumerate(zs):
            sp, lb = _softplus_parts(z + bias)
            if i == 0:
                sp = jnp.where(causal, sp, 0.0)
            sps.append(sp)
            lbs.append(lb)
        later = _dot(jnp.concatenate([sp.astype(BF16) for sp in sps], axis=0), upper)
        carry = jnp.zeros((nrow, 1), F32)
        a_blocks = []
        for i, (sp, lb) in enumerate(zip(sps, lbs)):
            a = jnp.exp(lb - later[i * nrow:(i + 1) * nrow] - carry)
            if i == 0:
                a = jnp.where(causal, a, 0.0)
            a_blocks.append(a.astype(BF16))
            carry = carry + jnp.sum(sp, axis=-1, keepdims=True)
        a_pages = jnp.concatenate(a_blocks[:0:-1], axis=1)
        acc = _dot(a_blocks[0], new_v) + _dot(a_pages, v_all)
        out = jnp.zeros((rows, SB_WIDTH), F32)
        for h in range(SB_HEADS):
            out = out + jnp.where(lane_head == h, acc[h * rows:(h + 1) * rows], 0.0)
        o_ref[s] = (out * _silu(g_ref[s])).astype(o_ref.dtype)


def _sb_sample(pa3, k3, v3, pb3, cache_k, cache_v, page_table, bias, layer):
    b = pa3.shape[0]
    nb = min(SB_NB_SAMPLE, b)
    n_pages = page_table.shape[1]
    page = cache_k.shape[3]
    tok = lambda off: pl.BlockSpec(
        (nb, SAMPLE_PAD, SB_WIDTH), lambda i, pt, off=off: (i, 0, off // SB_WIDTH))
    page_spec = lambda s, j: pl.BlockSpec(
        (None, None, SB_WIDTH, page), lambda i, pt, s=s, j=j: (layer, pt[nb * i + s, j], 0, 0))
    page_specs = [page_spec(s, j) for s in range(nb) for j in range(n_pages)]
    grid_spec = pltpu.PrefetchScalarGridSpec(
        num_scalar_prefetch=1,
        grid=(b // nb,),
        in_specs=[pl.BlockSpec(memory_space=pltpu.SMEM),
                  tok(A_SQ), tok(0), tok(0), tok(B_SG)] + page_specs * 2,
        out_specs=pl.BlockSpec((nb, SAMPLE_PAD, SB_WIDTH), lambda i, pt: (i, 0, 0)),
    )
    n_in = nb * n_pages
    return pl.pallas_call(
        functools.partial(_sb_sample_kernel, n_pages=n_pages, page=page),
        grid_spec=grid_spec,
        out_shape=jax.ShapeDtypeStruct((b, SAMPLE_PAD, SB_WIDTH), BF16),
        compiler_params=pltpu.CompilerParams(
            dimension_semantics=("arbitrary",), vmem_limit_bytes=VMEM_LIMIT),
        name="sb_sample",
    )(page_table, bias, pa3, k3, v3, pb3, *([cache_k] * n_in), *([cache_v] * n_in))


def _conv_tail(y, lnw_ref, lnb_ref, pw_ref, gate):
    mu = jnp.mean(y, axis=-1, keepdims=True)
    d = y - mu
    var = jnp.mean(d * d, axis=-1, keepdims=True)
    z = d * lax.rsqrt(var + EPS) * lnw_ref[...] + lnb_ref[...]
    return _dot(_silu(z).astype(BF16), pw_ref[...]) * _silu(gate)


def _conv_prompt_kernel(ca_ref, cb_ref, cg_ref, dww_ref, dwb_ref, lnw_ref, lnb_ref, pw_ref,
                        o_ref, st_ref, ext_scr, phase_scr):
    ti = pl.program_id(1)
    tt = ca_ref.shape[0]

    @pl.when(ti == 0)
    def _():
        ext_scr[0:CONV_HALO, :] = jnp.zeros((CONV_HALO, CONV_WIDTH), F32)

    @pl.when(ti > 0)
    def _():
        ext_scr[0:CONV_HALO, :] = ext_scr[tt:tt + CONV_HALO, :]

    u = ca_ref[...] * _sigmoid(cb_ref[...])
    ext_scr[CONV_HALO:CONV_HALO + tt, :] = u
    first_tap = CONV_HALO - (CONV_K - 1)
    for r in range(1, SUBLANES):
        phase_scr[r - 1] = ext_scr[r:r + phase_scr.shape[1], :]
    y = jnp.zeros((tt, CONV_WIDTH), F32) + dwb_ref[...]
    for j in range(CONV_K):
        r = (first_tap + j) % SUBLANES
        a = first_tap + j - r
        window = ext_scr[a:a + tt, :] if r == 0 else phase_scr[r - 1, a:a + tt, :]
        y = y + window * dww_ref[j:j + 1, :]
    o_ref[...] = _conv_tail(y, lnw_ref, lnb_ref, pw_ref, cg_ref[...]).astype(o_ref.dtype)

    @pl.when(ti == pl.num_programs(1) - 1)
    def _():
        st_ref[...] = ext_scr[CONV_HALO + tt - (CONV_K - 1):CONV_HALO + tt, :]


def _conv_prompt(p3, dw_w, dw_b, ln_w, ln_b, pw_bf16):
    b, t, _ = p3.shape
    tt = min(CONV_TT, t)
    col = lambda off: pl.BlockSpec(
        (None, tt, CONV_WIDTH), lambda i, c, off=off: (i, c, off // CONV_WIDTH))
    fixed = lambda a: pl.BlockSpec(a.shape, lambda i, c: (0, 0))
    return pl.pallas_call(
        _conv_prompt_kernel,
        grid=(b, t // tt),
        in_specs=[col(B_CA), col(B_CB), col(B_CG),
                  fixed(dw_w), fixed(dw_b), fixed(ln_w), fixed(ln_b), fixed(pw_bf16)],
        out_specs=[
            pl.BlockSpec((None, tt, CONV_WIDTH), lambda i, c: (i, c, 0)),
            pl.BlockSpec((None, CONV_K - 1, CONV_WIDTH), lambda i, c: (i, 0, 0)),
        ],
        out_shape=[
            jax.ShapeDtypeStruct((b, t, CONV_WIDTH), BF16),
            jax.ShapeDtypeStruct((b, CONV_K - 1, CONV_WIDTH), F32),
        ],
        scratch_shapes=[
            pltpu.VMEM((tt + CONV_HALO, CONV_WIDTH), F32),
            pltpu.VMEM((SUBLANES - 1, tt + CONV_HALO - SUBLANES, CONV_WIDTH), F32),
        ],
        compiler_params=pltpu.CompilerParams(
            dimension_semantics=("arbitrary", "arbitrary"), vmem_limit_bytes=VMEM_LIMIT),
        name="conv_prompt",
    )(p3, p3, p3, dw_w, dw_b, ln_w, ln_b, pw_bf16)


def _conv_sample_kernel(ca_ref, cb_ref, cg_ref, buf_ref, dww_ref, dwb_ref, lnw_ref, lnb_ref,
                        pw_ref, o_ref, st_ref, *, n_tok):
    hist = CONV_K - 1
    rows = [buf_ref[:, j, :] for j in range(hist)]
    rows += [ca_ref[:, t, :] * _sigmoid(cb_ref[:, t, :]) for t in range(n_tok)]
    for t in range(n_tok):
        y = jnp.zeros_like(rows[0]) + dwb_ref[...]
        for j in range(CONV_K):
            y = y + rows[t + j] * dww_ref[j:j + 1, :]
        o_ref[:, t, :] = _conv_tail(y, lnw_ref, lnb_ref, pw_ref, cg_ref[:, t, :]).astype(o_ref.dtype)
    for t in range(n_tok, SAMPLE_PAD):
        o_ref[:, t, :] = jnp.zeros(rows[0].shape, o_ref.dtype)
    for j in range(hist):
        st_ref[:, j, :] = rows[n_tok + j]


def _conv_sample(p3, buf, dw_w, dw_b, ln_w, ln_b, pw_bf16, n_tok):
    b = p3.shape[0]
    bb = min(32, b)
    col = lambda off: pl.BlockSpec(
        (bb, SAMPLE_PAD, CONV_WIDTH), lambda i, off=off: (i, 0, off // CONV_WIDTH))
    fixed = lambda a: pl.BlockSpec(a.shape, lambda i: (0, 0))
    st_spec = pl.BlockSpec((bb, CONV_K - 1, CONV_WIDTH), lambda i: (i, 0, 0))
    return pl.pallas_call(
        functools.partial(_conv_sample_kernel, n_tok=n_tok),
        grid=(b // bb,),
        in_specs=[col(B_CA), col(B_CB), col(B_CG), st_spec,
                  fixed(dw_w), fixed(dw_b), fixed(ln_w), fixed(ln_b), fixed(pw_bf16)],
        out_specs=[pl.BlockSpec((bb, SAMPLE_PAD, CONV_WIDTH), lambda i: (i, 0, 0)), st_spec],
        out_shape=[
            jax.ShapeDtypeStruct((b, SAMPLE_PAD, CONV_WIDTH), BF16),
            jax.ShapeDtypeStruct(buf.shape, F32),
        ],
        compiler_params=pltpu.CompilerParams(
            dimension_semantics=("arbitrary",), vmem_limit_bytes=VMEM_LIMIT),
        name="conv_sample",
    )(p3, p3, p3, buf, dw_w, dw_b, ln_w, ln_b, pw_bf16)


def kernel(x_prompt, x_sample, state_ret, cache_k, cache_v, state_conv, page_table, norm_pre, w_in,
           ret_gn_w, sb_bias, conv_dw_w, conv_dw_b, conv_ln_w, conv_ln_b, conv_pw_w, w_out, norm_post):
    bp, tp, d = x_prompt.shape
    bs, ts, _ = x_sample.shape
    depth = w_in.shape[0]
    n_pages = page_table.shape[1]
    page = cache_k.shape[2]
    past_len = n_pages * page
    assert ts <= SAMPLE_PAD and tp % RET_CHUNK == 0

    xp = x_prompt.reshape(bp * tp, d)
    xs = jnp.pad(x_sample, ((0, 0), (0, SAMPLE_PAD - ts), (0, 0))).reshape(bs * SAMPLE_PAD, d)
    ck = cache_k.transpose(0, 1, 3, 4, 2).reshape(depth, cache_k.shape[1], SB_WIDTH, page)
    cv = cache_v.transpose(0, 1, 3, 4, 2).reshape(depth, cache_v.shape[1], SB_WIDTH, page)

    cos_p, sin_p = _rope_tables(jnp.arange(tp))
    cos_s, sin_s = _rope_tables(past_len + jnp.arange(SAMPLE_PAD))
    tab_p = _retention_tables(RET_CHUNK, RET_CHUNK, RET_CHUNK)
    tab_s = _retention_tables(ts, SAMPLE_PAD, PAIR)
    s0_p = jnp.zeros((bp, RET_HEADS, HEAD_DIM, HEAD_DIM), F32)

    kt_buf = jnp.zeros((depth, bp, SB_WIDTH, tp), F32)
    vt_buf = jnp.zeros((depth, bp, SB_WIDTH, tp), F32)
    outs = [[] for _ in range(6)]
    for l in range(depth):
        w_in_b = w_in[l].astype(BF16)
        w_out_b = w_out[l].astype(BF16)
        pw_b = conv_pw_w[l].astype(BF16)
        g_pre, g_post = norm_pre[l][None, :], norm_post[l][None, :]
        gn_w = ret_gn_w[l][None, :]
        dw_b, ln_w, ln_b = conv_dw_b[l][None, :], conv_ln_w[l][None, :], conv_ln_b[l][None, :]

        pa, pb, kt_buf, vt_buf = _inproj_prompt(xp, g_pre, w_in_b, kt_buf, vt_buf, l)
        pa, pb = pa.reshape(bp, tp, -1), pb.reshape(bp, tp, -1)
        or_p, s_p = _retention(pa, cos_p, sin_p, tab_p, gn_w, s0_p, chunk=RET_CHUNK,
                               key_rows=RET_CHUNK, nb=RET_NB_PROMPT)
        os_p = _sb_prompt(pa, kt_buf, vt_buf, pb, sb_bias[l], l)
        oc_p, cst_p = _conv_prompt(pb, conv_dw_w[l], dw_b, ln_w, ln_b, pw_b)
        xp = _outproj(or_p.reshape(bp * tp, -1), os_p.reshape(bp * tp, -1), oc_p.reshape(bp * tp, -1),
                      xp, w_out_b, g_post)

        sa, k_s, v_s, sb = [a.reshape(bs, SAMPLE_PAD, -1) for a in _inproj(xs, g_pre, w_in_b)]
        or_s, s_s = _retention(sa, cos_s, sin_s, tab_s, gn_w, state_ret[l],
                               chunk=SAMPLE_PAD, key_rows=PAIR, nb=RET_NB_SAMPLE)
        os_s = _sb_sample(sa, k_s, v_s, sb, ck, cv, page_table, sb_bias[l], l)
        oc_s, cst_s = _conv_sample(sb, state_conv[l], conv_dw_w[l], dw_b, ln_w, ln_b, pw_b, ts)
        n_s = bs * SAMPLE_PAD
        xs = _outproj(or_s.reshape(n_s, -1), os_s.reshape(n_s, -1), oc_s.reshape(n_s, -1),
                      xs, w_out_b, g_post)

        outs[0].append(s_p)
        outs[1].append(s_s)
        outs[2].append(k_s[:, :ts].reshape(bs, ts, SB_HEADS, HEAD_DIM))
        outs[3].append(v_s[:, :ts].reshape(bs, ts, SB_HEADS, HEAD_DIM))
        outs[4].append(cst_p)
        outs[5].append(cst_s)

    y_prompt = xp.reshape(bp, tp, d)
    y_sample = xs.reshape(bs, SAMPLE_PAD, d)[:, :ts]
    ret_p, ret_s, k_rows_s, v_rows_s, conv_p, conv_s = [jnp.stack(o) for o in outs]
    rows_p = [a.reshape(depth, bp, SB_HEADS, HEAD_DIM, tp).transpose(0, 1, 4, 2, 3)
              for a in (kt_buf, vt_buf)]
    return (y_prompt, y_sample, ret_p, ret_s, rows_p[0], rows_p[1], k_rows_s, v_rows_s, conv_p, conv_s)
```

```python
import functools

import numpy as np
import jax
import jax.numpy as jnp
from jax import lax
from jax.experimental import pallas as pl
from jax.experimental.pallas import tpu as pltpu

F32 = jnp.float32
BF16 = jnp.bfloat16

HEAD_DIM = 64
PAIR = 2 * HEAD_DIM
RET_HEADS = 6
SB_HEADS = 6
RET_WIDTH = RET_HEADS * HEAD_DIM
SB_WIDTH = SB_HEADS * HEAD_DIM
CONV_WIDTH = 256
CONV_K = 31
IN_WIDTH = 4 * RET_WIDTH + 4 * SB_WIDTH + 3 * CONV_WIDTH
RET_CHUNK = 128
ROPE_BASE = 10000.0
EPS = 1e-6
LOG2_E = 1.4426950408889634
SUBLANES = 8
SAMPLE_PAD = SUBLANES

INPROJ_WIDTHS = (1920, SB_WIDTH, SB_WIDTH, 1152)
A_SQ = 1536
B_CA, B_CB, B_CG, B_SG = 0, 256, 512, 768
INPROJ_CHUNKS = ((0, 768, 0, 0), (768, 768, 0, 768), (1536, 384, 0, 1536), (1920, 384, 1, 0),
                 (2304, 384, 2, 0), (3072, 768, 3, 0), (2688, 384, 3, 768))

ROW_TILE = 512
OUT_ROW_TILE = 1024
SB_TQ = 512
SB_TK = 256
RET_NB_PROMPT = 2
RET_NB_SAMPLE = 8
SB_NB_SAMPLE = 1
CONV_TT = 512
CONV_HALO = 32
VMEM_LIMIT = 56 * 1024 * 1024


def _dot(a, b):
    return jnp.dot(a, b, preferred_element_type=F32)


def _dot_nt(a, b):
    return lax.dot_general(a, b, (((1,), (1,)), ((), ())), preferred_element_type=F32)


def _dot_tn(a, b):
    return lax.dot_general(a, b, (((0,), (0,)), ((), ())), preferred_element_type=F32)


def _sigmoid(x):
    return 1.0 / (1.0 + jnp.exp(-x))


def _silu(x):
    return x * _sigmoid(x)


def _inproj_kernel(x_ref, g_ref, w_ref, *out_refs):
    x = x_ref[...]
    ms = jnp.mean(x * x, axis=-1, keepdims=True)
    h = (x * lax.rsqrt(ms + EPS) * g_ref[...]).astype(BF16)
    for src, width, dst, col in INPROJ_CHUNKS:
        out_refs[dst][:, col:col + width] = _dot(h, w_ref[:, src:src + width])


def _inproj(x2d, g, w_bf16):
    n, d = x2d.shape
    tm = min(ROW_TILE, n)
    widths = INPROJ_WIDTHS
    return pl.pallas_call(
        _inproj_kernel,
        grid=(n // tm,),
        in_specs=[
            pl.BlockSpec((tm, d), lambda i: (i, 0)),
            pl.BlockSpec((1, d), lambda i: (0, 0)),
            pl.BlockSpec((d, IN_WIDTH), lambda i: (0, 0)),
        ],
        out_specs=[pl.BlockSpec((tm, w), lambda i: (i, 0)) for w in widths],
        out_shape=[jax.ShapeDtypeStruct((n, w), F32) for w in widths],
        compiler_params=pltpu.CompilerParams(
            dimension_semantics=("arbitrary",), vmem_limit_bytes=VMEM_LIMIT),
        name="inproj",
    )(x2d, g, w_bf16)


def _inproj_prompt_kernel(x_ref, g_ref, w_ref, wkt_ref, wvt_ref, kt_in, vt_in,
                          a_ref, b_ref, kt_ref, vt_ref):
    del kt_in, vt_in
    x = x_ref[...]
    ms = jnp.mean(x * x, axis=-1, keepdims=True)
    h = (x * lax.rsqrt(ms + EPS) * g_ref[...]).astype(BF16)
    for src, width, dst, col in INPROJ_CHUNKS:
        if dst == 0:
            a_ref[:, col:col + width] = _dot(h, w_ref[:, src:src + width])
        elif dst == 3:
            b_ref[:, col:col + width] = _dot(h, w_ref[:, src:src + width])
    kt_ref[...] = _dot_nt(wkt_ref[...], h)
    vt_ref[...] = _dot_nt(wvt_ref[...], h)


def _inproj_prompt(x2d, g, w_bf16, kt_buf, vt_buf, layer):
    n, d = x2d.shape
    t = kt_buf.shape[3]
    tm = min(ROW_TILE, t)
    per_seq = t // tm
    off_k, off_v = INPROJ_CHUNKS[3][0], INPROJ_CHUNKS[4][0]
    wkt = w_bf16[:, off_k:off_k + SB_WIDTH].T
    wvt = w_bf16[:, off_v:off_v + SB_WIDTH].T
    row = lambda i: (i, 0)
    fixed = lambda i: (0, 0)
    t_spec = pl.BlockSpec((None, None, SB_WIDTH, tm), lambda i: (layer, i // per_seq, 0, i % per_seq))
    return pl.pallas_call(
        _inproj_prompt_kernel,
        grid=(n // tm,),
        in_specs=[
            pl.BlockSpec((tm, d), row),
            pl.BlockSpec((1, d), fixed),
            pl.BlockSpec((d, IN_WIDTH), fixed),
            pl.BlockSpec(wkt.shape, fixed),
            pl.BlockSpec(wvt.shape, fixed),
            pl.BlockSpec(memory_space=pl.ANY),
            pl.BlockSpec(memory_space=pl.ANY),
        ],
        out_specs=[pl.BlockSpec((tm, INPROJ_WIDTHS[0]), row), pl.BlockSpec((tm, INPROJ_WIDTHS[3]), row),
                   t_spec, t_spec],
        out_shape=[jax.ShapeDtypeStruct((n, INPROJ_WIDTHS[0]), F32),
                   jax.ShapeDtypeStruct((n, INPROJ_WIDTHS[3]), F32),
                   jax.ShapeDtypeStruct(kt_buf.shape, F32), jax.ShapeDtypeStruct(vt_buf.shape, F32)],
        input_output_aliases={5: 2, 6: 3},
        compiler_params=pltpu.CompilerParams(
            dimension_semantics=("arbitrary",), vmem_limit_bytes=VMEM_LIMIT),
        name="inproj_prompt",
    )(x2d, g, w_bf16, wkt, wvt, kt_buf, vt_buf)


def _outproj_kernel(or_ref, os_ref, oc_ref, x_ref, w_ref, g_ref, o_ref):
    mix = jnp.concatenate([or_ref[...], os_ref[...], oc_ref[...]], axis=-1)
    y = _dot(mix, w_ref[...])
    ms = jnp.mean(y * y, axis=-1, keepdims=True)
    o_ref[...] = x_ref[...] + y * lax.rsqrt(ms + EPS) * g_ref[...]


def _outproj(o_r, o_s, o_c, x2d, w_bf16, g):
    n, d = x2d.shape
    tm = min(OUT_ROW_TILE, n)
    row = lambda i: (i, 0)
    fixed = lambda i: (0, 0)
    return pl.pallas_call(
        _outproj_kernel,
        grid=(n // tm,),
        in_specs=[
            pl.BlockSpec((tm, RET_WIDTH), row),
            pl.BlockSpec((tm, SB_WIDTH), row),
            pl.BlockSpec((tm, CONV_WIDTH), row),
            pl.BlockSpec((tm, d), row),
            pl.BlockSpec(w_bf16.shape, fixed),
            pl.BlockSpec((1, d), fixed),
        ],
        out_specs=pl.BlockSpec((tm, d), row),
        out_shape=jax.ShapeDtypeStruct((n, d), F32),
        compiler_params=pltpu.CompilerParams(
            dimension_semantics=("arbitrary",), vmem_limit_bytes=VMEM_LIMIT),
        name="outproj",
    )(o_r, o_s, o_c, x2d, w_bf16, g)


def _rope_tables(pos):
    half = HEAD_DIM // 2
    inv = jnp.power(ROPE_BASE, -2.0 * jnp.arange(half, dtype=F32) / HEAD_DIM)
    ang = pos.astype(F32)[:, None] * inv[None, :]
    cos, sin = jnp.cos(ang), jnp.sin(ang)
    cos_t = jnp.concatenate([cos, cos, cos, cos], axis=-1)
    sin_t = jnp.concatenate([-sin, sin, -sin, sin], axis=-1)
    return cos_t, sin_t


def _retention_tables(c_valid, rows, keys):
    h = np.arange(RET_HEADS, dtype=np.float64)
    lg = np.log(1.0 - np.power(2.0, -5.0 - h))
    c = np.arange(rows, dtype=np.float64)
    s = np.arange(keys, dtype=np.float64)
    diff = c[:, None] - s[None, :]
    ok = (diff >= 0) & (s[None, :] < c_valid) & (c[:, None] < c_valid)
    dmat = np.where(ok[None], np.exp(np.maximum(diff, 0.0)[None] * lg[:, None, None]), 0.0)
    q_dec = np.exp((c[:, None] + 1.0) * lg[None, :])
    k_dec = np.where(c[:, None] < c_valid, np.exp((c_valid - 1.0 - c)[:, None] * lg[None, :]), 0.0)
    c_dec = np.exp(c_valid * lg)[None, :]
    lanes = lambda t: np.repeat(t, HEAD_DIM, axis=-1)
    return (jnp.asarray(dmat, F32), jnp.asarray(lanes(q_dec), F32),
            jnp.asarray(lanes(k_dec), F32), jnp.asarray(lanes(c_dec), F32))


def _retention_kernel(rq_ref, rk_ref, rv_ref, rg_ref, cos_ref, sin_ref, d_ref, qdec_ref,
                      kdec_ref, cdec_ref, gn_ref, s0_ref, o_ref, s_out_ref, s_scr, *, key_rows):
    c_idx = pl.program_id(1)
    nb, rows = rq_ref.shape[0], rq_ref.shape[1]
    lane = lax.broadcasted_iota(jnp.int32, (rows, PAIR), 1)
    first = lane < HEAD_DIM
    low_half = (lane % HEAD_DIM) < (HEAD_DIM // 2)
    bd_r = lax.broadcasted_iota(jnp.int32, (PAIR, PAIR), 0) < HEAD_DIM
    bd_c = lax.broadcasted_iota(jnp.int32, (PAIR, PAIR), 1) < HEAD_DIM
    block_diag = bd_r == bd_c

    @pl.when(c_idx == 0)
    def _():
        for i, p in [(i, p) for i in range(nb) for p in range(RET_HEADS // 2)]:
            s_scr[i, p] = jnp.where(block_diag, s0_ref[i, p], 0.0)
    cos_t = cos_ref[...]
    sin_t = sin_ref[...]

    def rope(x):
        swapped = jnp.where(low_half, pltpu.roll(x, PAIR - HEAD_DIM // 2, 1),
                            pltpu.roll(x, HEAD_DIM // 2, 1))
        return x * cos_t + swapped * sin_t

    def pad_keys(x):
        if key_rows == rows:
            return x
        return jnp.concatenate([x, jnp.zeros((key_rows - rows, PAIR), x.dtype)], axis=0)

    for i, p in [(i, p) for i in range(nb) for p in range(RET_HEADS // 2)]:
        sl = slice(PAIR * p, PAIR * (p + 1))
        q = rope(rq_ref[i, :, sl])
        k = rope(rk_ref[i, :, sl]) * (HEAD_DIM ** -0.5)
        qb = q.astype(BF16)
        kb = pad_keys(k.astype(BF16))
        vb = pad_keys(rv_ref[i, :, sl].astype(BF16))
        zero = jnp.zeros_like(qb)
        att0 = _dot_nt(jnp.where(first, qb, zero), kb) * d_ref[2 * p]
        att1 = _dot_nt(jnp.where(first, zero, qb), kb) * d_ref[2 * p + 1]
        s_old = s_scr[i, p]
        o = jnp.where(first, _dot(att0.astype(BF16), vb), _dot(att1.astype(BF16), vb))
        o = o + _dot((q * qdec_ref[:, sl]).astype(BF16), s_old.astype(BF16))
        kd = pad_keys((k * kdec_ref[:, sl]).astype(BF16))
        kv = _dot_tn(kd, vb)
        s_scr[i, p] = s_old * cdec_ref[:, sl] + jnp.where(block_diag, kv, 0.0)

        inv_d = 1.0 / HEAD_DIM
        mu0 = jnp.sum(jnp.where(first, o, 0.0), axis=-1, keepdims=True) * inv_d
        mu1 = jnp.sum(jnp.where(first, 0.0, o), axis=-1, keepdims=True) * inv_d
        dlt = o - jnp.where(first, mu0, mu1)
        sq = dlt * dlt
        v0 = jnp.sum(jnp.where(first, sq, 0.0), axis=-1, keepdims=True) * inv_d
        v1 = jnp.sum(jnp.where(first, 0.0, sq), axis=-1, keepdims=True) * inv_d
        y = dlt * lax.rsqrt(jnp.where(first, v0, v1) + EPS) * gn_ref[:, sl]
        o_ref[i, :, sl] = (y * _silu(rg_ref[i, :, sl])).astype(o_ref.dtype)

    @pl.when(c_idx == pl.num_programs(1) - 1)
    def _():
        for i, p in [(i, p) for i in range(nb) for p in range(RET_HEADS // 2)]:
            s_bd = s_scr[i, p]
            stacked = jnp.where(bd_r, s_bd, pltpu.roll(s_bd, HEAD_DIM, 1))
            s_out_ref[i, p] = stacked[:, :HEAD_DIM]


def _retention(p3, cos_t, sin_t, tables, gn_w, s0, *, chunk, key_rows, nb):
    b, t, _ = p3.shape
    nc = t // chunk
    nb = min(nb, b)
    dmat, q_dec, k_dec, c_dec = tables
    pairs = RET_HEADS // 2
    s0 = s0.reshape(b, pairs, PAIR, HEAD_DIM)
    s0_wide = jnp.concatenate([s0, s0], axis=-1)
    col = lambda j: pl.BlockSpec((nb, chunk, RET_WIDTH), lambda i, c, j=j: (i, c, j))
    fixed2 = lambda a: pl.BlockSpec(a.shape, lambda i, c: (0, 0))
    state_spec = pl.BlockSpec((nb, pairs, PAIR, PAIR), lambda i, c: (i, 0, 0, 0))
    out_state_spec = pl.BlockSpec((nb, pairs, PAIR, HEAD_DIM), lambda i, c: (i, 0, 0, 0))
    o, s_new = pl.pallas_call(
        functools.partial(_retention_kernel, key_rows=key_rows),
        grid=(b // nb, nc),
        in_specs=[
            col(0), col(1), col(2), col(3),
            pl.BlockSpec((chunk, PAIR), lambda i, c: (c, 0)),
            pl.BlockSpec((chunk, PAIR), lambda i, c: (c, 0)),
            pl.BlockSpec(dmat.shape, lambda i, c: (0, 0, 0)),
            fixed2(q_dec), fixed2(k_dec), fixed2(c_dec), fixed2(gn_w),
            state_spec,
        ],
        out_specs=[
            pl.BlockSpec((nb, chunk, RET_WIDTH), lambda i, c: (i, c, 0)),
            out_state_spec,
        ],
        out_shape=[
            jax.ShapeDtypeStruct((b, t, RET_WIDTH), BF16),
            jax.ShapeDtypeStruct((b, pairs, PAIR, HEAD_DIM), F32),
        ],
        scratch_shapes=[pltpu.VMEM((nb, pairs, PAIR, PAIR), F32)],
        compiler_params=pltpu.CompilerParams(
            dimension_semantics=("arbitrary", "arbitrary"), vmem_limit_bytes=VMEM_LIMIT),
        name="retention",
    )(p3, p3, p3, p3, cos_t, sin_t, dmat, q_dec, k_dec, c_dec, gn_w, s0_wide)
    return o, s_new.reshape(b, RET_HEADS, HEAD_DIM, HEAD_DIM)


def _softplus_parts(z):
    u = jnp.exp2(jnp.minimum(z * LOG2_E, 126.0))
    sp = jnp.maximum(jnp.log(1.0 + u), z)
    return sp, z - sp


def _strict_upper(n):
    r = lax.broadcasted_iota(jnp.int32, (n, n), 0)
    c = lax.broadcasted_iota(jnp.int32, (n, n), 1)
    return jnp.where(r > c, 1.0, 0.0).astype(BF16)


def _sb_prompt_kernel(bias_ref, q_ref, k_ref, v_ref, g_ref, o_ref, kb_scr, vb_scr, acc_scr, car_scr,
                      z00_scr, z01_scr, z10_scr, z11_scr, w0_scr, w1_scr):
    hp = pl.program_id(1)
    qi = pl.program_id(2)
    tq, tk = SB_TQ, SB_TK

    @pl.when(qi == 0)
    def _():
        for c in range(kb_scr.shape[0]):
            kb_scr[c, :PAIR] = k_ref[:, c * tk:(c + 1) * tk].astype(BF16)
            ones = lax.broadcasted_iota(jnp.int32, (PAIR, tk), 0) < 2
            kb_scr[c, PAIR:] = jnp.where(ones, 1.0, 0.0).astype(BF16)
            vb_scr[c] = v_ref[:, c * tk:(c + 1) * tk].astype(BF16)

    q = q_ref[...] * (HEAD_DIM ** -0.5)
    lane = lax.broadcasted_iota(jnp.int32, (tq, PAIR), 1)
    first = lane < HEAD_DIM
    zero = jnp.zeros((tq, PAIR), F32)
    qs = jnp.concatenate([jnp.where(first, q, zero), jnp.where(first, zero, q)], axis=0)
    row2 = lax.broadcasted_iota(jnp.int32, (2 * tq, PAIR), 0)
    lane2 = lax.broadcasted_iota(jnp.int32, (2 * tq, PAIR), 1)
    bias2 = jnp.where(row2 < tq, bias_ref[2 * hp], bias_ref[2 * hp + 1])
    b_hi = bias2.astype(BF16).astype(F32)
    b_ext = jnp.where(lane2 == 0, b_hi, jnp.where(lane2 == 1, bias2 - b_hi, 0.0))
    q2 = jnp.concatenate([qs.astype(BF16), b_ext.astype(BF16)], axis=1)
    upper = _strict_upper(tk)
    col = lax.broadcasted_iota(jnp.int32, (2 * tq, tk), 1)
    row = lax.broadcasted_iota(jnp.int32, (2 * tq, tk), 0) % tq
    acc_scr[...] = jnp.zeros_like(acc_scr)
    car_scr[...] = jnp.zeros_like(car_scr)

    def scores(tiles):
        return [_dot(q2, kb_scr[j]) for j in tiles]

    def weights(zs, masks):
        carry = car_scr[...]
        sps, lbs, carries = [], [], []
        for z, mask in zip(zs, masks):
            sp, lb = _softplus_parts(z)
            if mask is not None:
                sp = jnp.where(mask, sp, 0.0)
            carries.append(carry)
            carry = carry + jnp.sum(sp, axis=-1, keepdims=True)
            sps.append(sp)
            lbs.append(lb)
        car_scr[...] = carry
        later = [_dot(sp.astype(BF16), upper) for sp in sps]
        out = []
        for lb, lt, cr, mask in zip(lbs, later, carries, masks):
            a = jnp.exp(lb - lt - cr)
            if mask is not None:
                a = jnp.where(mask, a, 0.0)
            out.append(a.astype(BF16))
        return out

    def accumulate(ws, tiles):
        acc = acc_scr[...]
        for w, j in zip(ws, tiles):
            acc = acc + _dot_nt(w, vb_scr[j])
        acc_scr[...] = acc

    n_pairs = qi
    top = 2 * qi - 1
    z_sets = ((z00_scr, z01_scr), (z10_scr, z11_scr))
    w_sets = (w0_scr, w1_scr)
    for z_ref, z in zip(z_sets[0], scores([jnp.maximum(top, 0), jnp.maximum(top - 1, 0)])):
        z_ref[...] = z

    diag = [2 * qi + 1, 2 * qi]
    masks = [(col + tk) < row, col < row]
    for t, w in enumerate(weights(scores(diag), masks)):
        w_sets[1][t] = w

    def step(i, cur, oth):
        j = top - 2 * i
        prev = [j + 2, j + 1]
        nxt = [jnp.maximum(j - 2, 0), jnp.maximum(j - 3, 0)]
        for z_ref, z in zip(z_sets[oth], scores(nxt)):
            z_ref[...] = z
        accumulate([w_sets[oth][0], w_sets[oth][1]], prev)
        row_carry = car_scr[...]
        args = []
        for z_ref in z_sets[cur]:
            sp, lb = _softplus_parts(z_ref[...])
            later = _dot(sp.astype(BF16), upper)
            args.append((lb, later, row_carry))
            row_carry = row_carry + jnp.sum(sp, axis=-1, keepdims=True)
        car_scr[...] = row_carry
        for t, (lb, later, cr) in enumerate(args):
            w_sets[cur][t] = jnp.exp(lb - later - cr).astype(BF16)

    def body(k, carry):
        step(2 * k, 0, 1)
        step(2 * k + 1, 1, 0)
        return carry

    lax.fori_loop(0, n_pairs // 2, body, 0)

    @pl.when(n_pairs % 2 == 1)
    def _():
        step(n_pairs - 1, 0, 1)
        accumulate([w_sets[0][0], w_sets[0][1]], [1, 0])

    @pl.when(n_pairs % 2 == 0)
    def _():
        accumulate([w_sets[1][0], w_sets[1][1]], [1, 0])

    acc = acc_scr[...]
    o = jnp.where(first, acc[:tq], acc[tq:])
    o_ref[...] = (o * _silu(g_ref[...])).astype(o_ref.dtype)


def _sb_prompt(pa3, kt_buf, vt_buf, pb3, bias, layer):
    b, t, _ = pa3.shape
    assert SB_TQ == 2 * SB_TK and t % SB_TQ == 0
    nq = t // SB_TQ
    pairs = SB_HEADS // 2
    qcol = lambda off: pl.BlockSpec(
        (None, SB_TQ, PAIR), lambda i, hp, qi, off=off: (i, qi, off // PAIR + hp))
    kt_spec = pl.BlockSpec((None, None, PAIR, t), lambda i, hp, qi: (layer, i, hp, 0))
    return pl.pallas_call(
        _sb_prompt_kernel,
        grid=(b, pairs, nq),
        in_specs=[
            pl.BlockSpec(memory_space=pltpu.SMEM),
            qcol(A_SQ), kt_spec, kt_spec, qcol(B_SG),
        ],
        out_specs=pl.BlockSpec((None, SB_TQ, PAIR), lambda i, hp, qi: (i, qi, hp)),
        out_shape=jax.ShapeDtypeStruct((b, t, SB_WIDTH), BF16),
        scratch_shapes=[
            pltpu.VMEM((t // SB_TK, 2 * PAIR, SB_TK), BF16), pltpu.VMEM((t // SB_TK, PAIR, SB_TK), BF16),
            pltpu.VMEM((2 * SB_TQ, PAIR), F32), pltpu.VMEM((2 * SB_TQ, 1), F32),
        ] + [pltpu.VMEM((2 * SB_TQ, SB_TK), F32)] * 4 + [pltpu.VMEM((2, 2 * SB_TQ, SB_TK), BF16)] * 2,
        compiler_params=pltpu.CompilerParams(
            dimension_semantics=("arbitrary", "arbitrary", "arbitrary"),
            vmem_limit_bytes=VMEM_LIMIT),
        name="sb_prompt",
    )(bias, pa3, kt_buf, vt_buf, pb3)


def _sb_sample_kernel(pt_ref, bias_ref, q_ref, kn_ref, vn_ref, g_ref, *rest, n_pages, page):
    del pt_ref
    nb = q_ref.shape[0]
    k_refs = rest[:nb * n_pages]
    v_refs = rest[nb * n_pages:2 * nb * n_pages]
    o_ref = rest[2 * nb * n_pages]
    rows = SAMPLE_PAD
    nrow = SB_HEADS * rows
    lane_head = lax.broadcasted_iota(jnp.int32, (rows, SB_WIDTH), 1) // HEAD_DIM
    zero = jnp.zeros((rows, SB_WIDTH), F32)
    row = lax.broadcasted_iota(jnp.int32, (nrow, page), 0)
    col = lax.broadcasted_iota(jnp.int32, (nrow, page), 1)
    bias = jnp.zeros((nrow, page), F32)
    for h in range(SB_HEADS):
        bias = jnp.where(row // rows == h, bias_ref[h], bias)
    causal = col < (row % rows)
    upper = _strict_upper(page)
    pad = jnp.zeros((page - rows, SB_WIDTH), BF16)

    for s in range(nb):
        q = q_ref[s] * (HEAD_DIM ** -0.5)
        qbd = jnp.concatenate(
            [jnp.where(lane_head == h, q, zero) for h in range(SB_HEADS)], axis=0).astype(BF16)
        new_k = jnp.concatenate([kn_ref[s].astype(BF16), pad], axis=0)
        new_v = jnp.concatenate([vn_ref[s].astype(BF16), pad], axis=0)
        k_all = jnp.concatenate(
            [k_refs[s * n_pages + j][...].astype(BF16) for j in range(n_pages)], axis=1)
        v_all = jnp.concatenate(
            [v_refs[s * n_pages + j][...].T.astype(BF16) for j in range(n_pages)], axis=0)
        z_pages = _dot(qbd, k_all)
        zs = [_dot_nt(qbd, new_k)]
        for j in reversed(range(n_pages)):
            zs.append(z_pages[:, j * page:(j + 1) * page])

        sps, lbs = [], []
        for i, z in enumerate(zs):
            sp, lb = _softplus_parts(z + bias)
            if i == 0:
                sp = jnp.where(causal, sp, 0.0)
            sps.append(sp)
            lbs.append(lb)
        later = _dot(jnp.concatenate([sp.astype(BF16) for sp in sps], axis=0), upper)
        carry = jnp.zeros((nrow, 1), F32)
        a_blocks = []
        for i, (sp, lb) in enumerate(zip(sps, lbs)):
            a = jnp.exp(lb - later[i * nrow:(i + 1) * nrow] - carry)
            if i == 0:
                a = jnp.where(causal, a, 0.0)
            a_blocks.append(a.astype(BF16))
            carry = carry + jnp.sum(sp, axis=-1, keepdims=True)
        a_pages = jnp.concatenate(a_blocks[:0:-1], axis=1)
        acc = _dot(a_blocks[0], new_v) + _dot(a_pages, v_all)
        out = jnp.zeros((rows, SB_WIDTH), F32)
        for h in range(SB_HEADS):
            out = out + jnp.where(lane_head == h, acc[h * rows:(h + 1) * rows], 0.0)
        o_ref[s] = (out * _silu(g_ref[s])).astype(o_ref.dtype)


def _sb_sample(pa3, k3, v3, pb3, cache_k, cache_v, page_table, bias, layer):
    b = pa3.shape[0]
    nb = min(SB_NB_SAMPLE, b)
    n_pages = page_table.shape[1]
    page = cache_k.shape[3]
    tok = lambda off: pl.BlockSpec(
        (nb, SAMPLE_PAD, SB_WIDTH), lambda i, pt, off=off: (i, 0, off // SB_WIDTH))
    page_spec = lambda s, j: pl.BlockSpec(
        (None, None, SB_WIDTH, page), lambda i, pt, s=s, j=j: (layer, pt[nb * i + s, j], 0, 0))
    page_specs = [page_spec(s, j) for s in range(nb) for j in range(n_pages)]
    grid_spec = pltpu.PrefetchScalarGridSpec(
        num_scalar_prefetch=1,
        grid=(b // nb,),
        in_specs=[pl.BlockSpec(memory_space=pltpu.SMEM),
                  tok(A_SQ), tok(0), tok(0), tok(B_SG)] + page_specs * 2,
        out_specs=pl.BlockSpec((nb, SAMPLE_PAD, SB_WIDTH), lambda i, pt: (i, 0, 0)),
    )
    n_in = nb * n_pages
    return pl.pallas_call(
        functools.partial(_sb_sample_kernel, n_pages=n_pages, page=page),
        grid_spec=grid_spec,
        out_shape=jax.ShapeDtypeStruct((b, SAMPLE_PAD, SB_WIDTH), BF16),
        compiler_params=pltpu.CompilerParams(
            dimension_semantics=("arbitrary",), vmem_limit_bytes=VMEM_LIMIT),
        name="sb_sample",
    )(page_table, bias, pa3, k3, v3, pb3, *([cache_k] * n_in), *([cache_v] * n_in))


def _conv_tail(y, lnw_ref, lnb_ref, pw_ref, gate):
    mu = jnp.mean(y, axis=-1, keepdims=True)
    d = y - mu
    var = jnp.mean(d * d, axis=-1, keepdims=True)
    z = d * lax.rsqrt(var + EPS) * lnw_ref[...] + lnb_ref[...]
    return _dot(_silu(z).astype(BF16), pw_ref[...]) * _silu(gate)


def _conv_prompt_kernel(ca_ref, cb_ref, cg_ref, dww_ref, dwb_ref, lnw_ref, lnb_ref, pw_ref,
                        o_ref, st_ref, ext_scr, phase_scr):
    ti = pl.program_id(1)
    tt = ca_ref.shape[0]

    @pl.when(ti == 0)
    def _():
        ext_scr[0:CONV_HALO, :] = jnp.zeros((CONV_HALO, CONV_WIDTH), F32)

    @pl.when(ti > 0)
    def _():
        ext_scr[0:CONV_HALO, :] = ext_scr[tt:tt + CONV_HALO, :]

    u = ca_ref[...] * _sigmoid(cb_ref[...])
    ext_scr[CONV_HALO:CONV_HALO + tt, :] = u
    first_tap = CONV_HALO - (CONV_K - 1)
    for r in range(1, SUBLANES):
        phase_scr[r - 1] = ext_scr[r:r + phase_scr.shape[1], :]
    y = jnp.zeros((tt, CONV_WIDTH), F32) + dwb_ref[...]
    for j in range(CONV_K):
        r = (first_tap + j) % SUBLANES
        a = first_tap + j - r
        window = ext_scr[a:a + tt, :] if r == 0 else phase_scr[r - 1, a:a + tt, :]
        y = y + window * dww_ref[j:j + 1, :]
    o_ref[...] = _conv_tail(y, lnw_ref, lnb_ref, pw_ref, cg_ref[...]).astype(o_ref.dtype)

    @pl.when(ti == pl.num_programs(1) - 1)
    def _():
        st_ref[...] = ext_scr[CONV_HALO + tt - (CONV_K - 1):CONV_HALO + tt, :]


def _conv_prompt(p3, dw_w, dw_b, ln_w, ln_b, pw_bf16):
    b, t, _ = p3.shape
    tt = min(CONV_TT, t)
    col = lambda off: pl.BlockSpec(
        (None, tt, CONV_WIDTH), lambda i, c, off=off: (i, c, off // CONV_WIDTH))
    fixed = lambda a: pl.BlockSpec(a.shape, lambda i, c: (0, 0))
    return pl.pallas_call(
        _conv_prompt_kernel,
        grid=(b, t // tt),
        in_specs=[col(B_CA), col(B_CB), col(B_CG),
                  fixed(dw_w), fixed(dw_b), fixed(ln_w), fixed(ln_b), fixed(pw_bf16)],
        out_specs=[
            pl.BlockSpec((None, tt, CONV_WIDTH), lambda i, c: (i, c, 0)),
            pl.BlockSpec((None, CONV_K - 1, CONV_WIDTH), lambda i, c: (i, 0, 0)),
        ],
        out_shape=[
            jax.ShapeDtypeStruct((b, t, CONV_WIDTH), BF16),
            jax.ShapeDtypeStruct((b, CONV_K - 1, CONV_WIDTH), F32),
        ],
        scratch_shapes=[
            pltpu.VMEM((tt + CONV_HALO, CONV_WIDTH), F32),
            pltpu.VMEM((SUBLANES - 1, tt + CONV_HALO - SUBLANES, CONV_WIDTH), F32),
        ],
        compiler_params=pltpu.CompilerParams(
            dimension_semantics=("arbitrary", "arbitrary"), vmem_limit_bytes=VMEM_LIMIT),
        name="conv_prompt",
    )(p3, p3, p3, dw_w, dw_b, ln_w, ln_b, pw_bf16)


def _conv_sample_kernel(ca_ref, cb_ref, cg_ref, buf_ref, dww_ref, dwb_ref, lnw_ref, lnb_ref,
                        pw_ref, o_ref, st_ref, *, n_tok):
    hist = CONV_K - 1
    rows = [buf_ref[:, j, :] for j in range(hist)]
    rows += [ca_ref[:, t, :] * _sigmoid(cb_ref[:, t, :]) for t in range(n_tok)]
    for t in range(n_tok):
        y = jnp.zeros_like(rows[0]) + dwb_ref[...]
        for j in range(CONV_K):
            y = y + rows[t + j] * dww_ref[j:j + 1, :]
        o_ref[:, t, :] = _conv_tail(y, lnw_ref, lnb_ref, pw_ref, cg_ref[:, t, :]).astype(o_ref.dtype)
    for t in range(n_tok, SAMPLE_PAD):
        o_ref[:, t, :] = jnp.zeros(rows[0].shape, o_ref.dtype)
    for j in range(hist):
        st_ref[:, j, :] = rows[n_tok + j]


def _conv_sample(p3, buf, dw_w, dw_b, ln_w, ln_b, pw_bf16, n_tok):
    b = p3.shape[0]
    bb = min(32, b)
    col = lambda off: pl.BlockSpec(
        (bb, SAMPLE_PAD, CONV_WIDTH), lambda i, off=off: (i, 0, off // CONV_WIDTH))
    fixed = lambda a: pl.BlockSpec(a.shape, lambda i: (0, 0))
    st_spec = pl.BlockSpec((bb, CONV_K - 1, CONV_WIDTH), lambda i: (i, 0, 0))
    return pl.pallas_call(
        functools.partial(_conv_sample_kernel, n_tok=n_tok),
        grid=(b // bb,),
        in_specs=[col(B_CA), col(B_CB), col(B_CG), st_spec,
                  fixed(dw_w), fixed(dw_b), fixed(ln_w), fixed(ln_b), fixed(pw_bf16)],
        out_specs=[pl.BlockSpec((bb, SAMPLE_PAD, CONV_WIDTH), lambda i: (i, 0, 0)), st_spec],
        out_shape=[
            jax.ShapeDtypeStruct((b, SAMPLE_PAD, CONV_WIDTH), BF16),
            jax.ShapeDtypeStruct(buf.shape, F32),
        ],
        compiler_params=pltpu.CompilerParams(
            dimension_semantics=("arbitrary",), vmem_limit_bytes=VMEM_LIMIT),
        name="conv_sample",
    )(p3, p3, p3, buf, dw_w, dw_b, ln_w, ln_b, pw_bf16)


def kernel(x_prompt, x_sample, state_ret, cache_k, cache_v, state_conv, page_table, norm_pre, w_in,
           ret_gn_w, sb_bias, conv_dw_w, conv_dw_b, conv_ln_w, conv_ln_b, conv_pw_w, w_out, norm_post):
    bp, tp, d = x_prompt.shape
    bs, ts, _ = x_sample.shape
    depth = w_in.shape[0]
    n_pages = page_table.shape[1]
    page = cache_k.shape[2]
    past_len = n_pages * page
    assert ts <= SAMPLE_PAD and tp % RET_CHUNK == 0

    xp = x_prompt.reshape(bp * tp, d)
    xs = jnp.pad(x_sample, ((0, 0), (0, SAMPLE_PAD - ts), (0, 0))).reshape(bs * SAMPLE_PAD, d)
    ck = cache_k.transpose(0, 1, 3, 4, 2).reshape(depth, cache_k.shape[1], SB_WIDTH, page)
    cv = cache_v.transpose(0, 1, 3, 4, 2).reshape(depth, cache_v.shape[1], SB_WIDTH, page)

    cos_p, sin_p = _rope_tables(jnp.arange(tp))
    cos_s, sin_s = _rope_tables(past_len + jnp.arange(SAMPLE_PAD))
    tab_p = _retention_tables(RET_CHUNK, RET_CHUNK, RET_CHUNK)
    tab_s = _retention_tables(ts, SAMPLE_PAD, PAIR)
    s0_p = jnp.zeros((bp, RET_HEADS, HEAD_DIM, HEAD_DIM), F32)

    kt_buf = jnp.zeros((depth, bp, SB_WIDTH, tp), F32)
    vt_buf = jnp.zeros((depth, bp, SB_WIDTH, tp), F32)
    outs = [[] for _ in range(6)]
    for l in range(depth):
        w_in_b = w_in[l].astype(BF16)
        w_out_b = w_out[l].astype(BF16)
        pw_b = conv_pw_w[l].astype(BF16)
        g_pre, g_post = norm_pre[l][None, :], norm_post[l][None, :]
        gn_w = ret_gn_w[l][None, :]
        dw_b, ln_w, ln_b = conv_dw_b[l][None, :], conv_ln_w[l][None, :], conv_ln_b[l][None, :]

        pa, pb, kt_buf, vt_buf = _inproj_prompt(xp, g_pre, w_in_b, kt_buf, vt_buf, l)
        pa, pb = pa.reshape(bp, tp, -1), pb.reshape(bp, tp, -1)
        or_p, s_p = _retention(pa, cos_p, sin_p, tab_p, gn_w, s0_p, chunk=RET_CHUNK,
                               key_rows=RET_CHUNK, nb=RET_NB_PROMPT)
        os_p = _sb_prompt(pa, kt_buf, vt_buf, pb, sb_bias[l], l)
        oc_p, cst_p = _conv_prompt(pb, conv_dw_w[l], dw_b, ln_w, ln_b, pw_b)
        xp = _outproj(or_p.reshape(bp * tp, -1), os_p.reshape(bp * tp, -1), oc_p.reshape(bp * tp, -1),
                      xp, w_out_b, g_post)

        sa, k_s, v_s, sb = [a.reshape(bs, SAMPLE_PAD, -1) for a in _inproj(xs, g_pre, w_in_b)]
        or_s, s_s = _retention(sa, cos_s, sin_s, tab_s, gn_w, state_ret[l],
                               chunk=SAMPLE_PAD, key_rows=PAIR, nb=RET_NB_SAMPLE)
        os_s = _sb_sample(sa, k_s, v_s, sb, ck, cv, page_table, sb_bias[l], l)
        oc_s, cst_s = _conv_sample(sb, state_conv[l], conv_dw_w[l], dw_b, ln_w, ln_b, pw_b, ts)
        n_s = bs * SAMPLE_PAD
        xs = _outproj(or_s.reshape(n_s, -1), os_s.reshape(n_s, -1), oc_s.reshape(n_s, -1),
                      xs, w_out_b, g_post)

        outs[0].append(s_p)
        outs[1].append(s_s)
        outs[2].append(k_s[:, :ts].reshape(bs, ts, SB_HEADS, HEAD_DIM))
        outs[3].append(v_s[:, :ts].reshape(bs, ts, SB_HEADS, HEAD_DIM))
        outs[4].append(cst_p)
        outs[5].append(cst_s)

    y_prompt = xp.reshape(bp, tp, d)
    y_sample = xs.reshape(bs, SAMPLE_PAD, d)[:, :ts]
    ret_p, ret_s, k_rows_s, v_rows_s, conv_p, conv_s = [jnp.stack(o) for o in outs]
    rows_p = [a.reshape(depth, bp, SB_HEADS, HEAD_DIM, tp).transpose(0, 1, 4, 2, 3)
              for a in (kt_buf, vt_buf)]
    return (y_prompt, y_sample, ret_p, ret_s, rows_p[0], rows_p[1], k_rows_s, v_rows_s, conv_p, conv_s)
```
